```python
import math
import jax, jax.numpy as jnp
from jax import lax
import numpy as np

D_MODEL = 1024
BATCH = 2
SEQ = 8192
DEPTH = 1

N_META = 16
MLA_HEADS = 8
QK_NOPE_DIM = 128
QK_ROPE_DIM = 64
V_HEAD_DIM = 128
Q_LORA_RANK = 384
KV_LORA_RANK = 256
ROPE_THETA = 10000.0
Q_BLOCK = 128
SOFTMAX_SCALE = (QK_NOPE_DIM + QK_ROPE_DIM) ** -0.5
D_ATTN = MLA_HEADS * V_HEAD_DIM
SSM_HEADS = 16
SSM_HEAD_DIM = 64
SSM_GROUPS = 2
HEADS_PER_GROUP = SSM_HEADS // SSM_GROUPS
SSM_STATE = 128
SSM_CONV = 4
CHUNK = 128
D_SSM = SSM_HEADS * SSM_HEAD_DIM
D_XBC = D_SSM + 2 * SSM_GROUPS * SSM_STATE
D_MIX = D_ATTN + D_SSM
D_FF = 2816
FFN_CONV = 3
EPS = 1e-6
IN_SPLITS = (Q_LORA_RANK, KV_LORA_RANK, QK_ROPE_DIM, D_SSM, D_XBC, SSM_HEADS)
D_IN = sum(IN_SPLITS)

kernel_name = 'hymba_mla_ssd_convffn_layer'


def rms_norm(x, gain):
    xf = x.astype(jnp.float32)
    y = xf * lax.rsqrt(jnp.mean(xf * xf, axis=-1, keepdims=True) + EPS)
    return (y * gain.astype(jnp.float32)).astype(x.dtype)


def causal_dwconv(x, w, b):
    k = w.shape[0]
    y = lax.conv_general_dilated(x, w[:, None, :].astype(x.dtype), window_strides=(1,),
                                 padding=[(k - 1, 0)], dimension_numbers=('NWC', 'WIO', 'NWC'),
                                 feature_group_count=x.shape[-1])
    return y + b.astype(x.dtype)


def split_cols(t, sizes):
    idx = np.cumsum(sizes)[:-1].tolist()
    return jnp.split(t, idx, axis=-1)


def rope_tables(n):
    inv = ROPE_THETA ** (-jnp.arange(0, QK_ROPE_DIM, 2, dtype=jnp.float32) / QK_ROPE_DIM)
    ang = jnp.arange(n, dtype=jnp.float32)[:, None] * inv[None, :]
    return jnp.cos(ang), jnp.sin(ang)


def apply_rope(x, cos, sin):
    x1, x2 = jnp.split(x, 2, axis=-1)
    cos = cos.astype(x.dtype)
    sin = sin.astype(x.dtype)
    return jnp.concatenate([x1 * cos - x2 * sin, x1 * sin + x2 * cos], axis=-1)


def attend_block(q_nope, q_pe, q_pos, k_nope, k_pe, v, k_pos):
    s = jnp.einsum('bqhd,bkhd->bhqk', q_nope, k_nope) + jnp.einsum('bqhr,bkr->bhqk', q_pe, k_pe)
    s = s.astype(jnp.float32) * SOFTMAX_SCALE
    s = jnp.where(k_pos[None, :] <= q_pos[:, None], s, -jnp.inf)
    p = jax.nn.softmax(s, axis=-1).astype(v.dtype)
    return jnp.einsum('bhqk,bkhd->bqhd', p, v)


def mla_group(q_c, kv_c, k_pe_raw, q_a_norm, w_uq, kv_a_norm, w_ukv, cos, sin):
    bsz, L = q_c.shape[0], q_c.shape[1]
    S = L - N_META
    q = jnp.einsum('blr,rhd->blhd', rms_norm(q_c, q_a_norm), w_uq)
    q_nope = q[..., :QK_NOPE_DIM]
    q_pe = apply_rope(q[..., QK_NOPE_DIM:], cos[:, None, :], sin[:, None, :])
    kv = jnp.einsum('blc,chd->blhd', rms_norm(kv_c, kv_a_norm), w_ukv)
    k_nope, v = kv[..., :QK_NOPE_DIM], kv[..., QK_NOPE_DIM:]
    k_pe = apply_rope(k_pe_raw, cos, sin)
    pos = jnp.arange(L)
    o_meta = attend_block(q_nope[:, :N_META], q_pe[:, :N_META], pos[:N_META],
                          k_nope[:, :N_META], k_pe[:, :N_META], v[:, :N_META], pos[:N_META])
    nb = S // Q_BLOCK

    def to_blocks(t):
        t = t[:, N_META:].reshape((bsz, nb, Q_BLOCK) + t.shape[2:])
        return jnp.moveaxis(t, 1, 0)

    o_real = lax.map(lambda blk: attend_block(blk[0], blk[1], blk[2], k_nope, k_pe, v, pos),
                     (to_blocks(q_nope), to_blocks(q_pe), pos[N_META:].reshape(nb, Q_BLOCK)))
    o_real = jnp.moveaxis(o_real, 0, 1).reshape(bsz, S, MLA_HEADS, V_HEAD_DIM)
    o = jnp.concatenate([o_meta, o_real], axis=1)
    return o.reshape(bsz, L, D_ATTN)


def ssd_chunks(xs, dt, A, Bm, Cm, init):
    Acs = jnp.cumsum(dt * A, axis=2)
    Q = xs.shape[2]
    causal = jnp.tril(jnp.ones((Q, Q), dtype=bool))[:, :, None, None]
    seg = Acs[:, :, :, None] - Acs[:, :, None, :]
    decay_in = jnp.exp(jnp.where(causal, seg, -jnp.inf))
    xdt = xs * dt[..., None]
    cb = jnp.einsum('bclgn,bcsgn->bclsg', Cm, Bm)
    y_diag = jnp.einsum('bclsgh,bcsghp->bclghp', cb[..., None] * decay_in, xdt)
    decay_to_end = jnp.exp(Acs[:, :, -1:] - Acs)
    states = jnp.einsum('bcsgn,bcsghp->bcghpn', Bm, xdt * decay_to_end[..., None])
    chunk_decay = jnp.exp(Acs[:, :, -1])

    def step(carry, inp):
        st, dec = inp
        return carry * dec[..., None, None] + st, carry

    final, prev = lax.scan(step, init, (jnp.moveaxis(states, 1, 0), jnp.moveaxis(chunk_decay, 1, 0)))
    prev = jnp.moveaxis(prev, 0, 1)
    y_off = jnp.einsum('bclgn,bcghpn->bclghp', Cm, prev) * jnp.exp(Acs)[..., None]
    return y_diag + y_off, final


def ssd_group(z, xbc, dt_raw, conv_w, conv_b, dt_bias, A_log, D, norm_gain):
    bsz, L = z.shape[0], z.shape[1]
    S = L - N_META
    nc = S // CHUNK
    f32 = jnp.float32
    xbc = jax.nn.silu(causal_dwconv(xbc, conv_w, conv_b)).astype(f32)
    xs, Bm, Cm = split_cols(xbc, (D_SSM, SSM_GROUPS * SSM_STATE, SSM_GROUPS * SSM_STATE))
    xs = xs.reshape(bsz, L, SSM_GROUPS, HEADS_PER_GROUP, SSM_HEAD_DIM)
    Bm = Bm.reshape(bsz, L, SSM_GROUPS, SSM_STATE)
    Cm = Cm.reshape(bsz, L, SSM_GROUPS, SSM_STATE)
    dt = jax.nn.softplus(dt_raw.astype(f32) + dt_bias.astype(f32)).reshape(bsz, L, SSM_GROUPS, HEADS_PER_GROUP)
    A = -jnp.exp(A_log.astype(f32)).reshape(SSM_GROUPS, HEADS_PER_GROUP)

    def meta_part(t):
        return t[:, None, :N_META]

    def real_part(t):
        return t[:, N_META:].reshape((bsz, nc, CHUNK) + t.shape[2:])

    init = jnp.zeros((bsz, SSM_GROUPS, HEADS_PER_GROUP, SSM_HEAD_DIM, SSM_STATE), f32)
    y_meta, h_meta = ssd_chunks(meta_part(xs), meta_part(dt), A, meta_part(Bm), meta_part(Cm), init)
    y_real, _ = ssd_chunks(real_part(xs), real_part(dt), A, real_part(Bm), real_part(Cm), h_meta)
    y = jnp.concatenate([y_meta.reshape(bsz, N_META, SSM_GROUPS, HEADS_PER_GROUP, SSM_HEAD_DIM),
                         y_real.reshape(bsz, S, SSM_GROUPS, HEADS_PER_GROUP, SSM_HEAD_DIM)], axis=1)
    y = y + D.astype(f32).reshape(SSM_GROUPS, HEADS_PER_GROUP)[..., None] * xs
    gsz = HEADS_PER_GROUP * SSM_HEAD_DIM
    y = y.reshape(bsz, L, SSM_GROUPS, gsz) * jax.nn.silu(z.astype(f32)).reshape(bsz, L, SSM_GROUPS, gsz)
    y = y * lax.rsqrt(jnp.mean(y * y, axis=-1, keepdims=True) + EPS)
    return (y.reshape(bsz, L, D_SSM) * norm_gain.astype(f32)).astype(z.dtype)


def conv_ffn(h, w_up, conv_w, conv_b, w_down):
    u = causal_dwconv(h @ w_up, conv_w, conv_b)
    g, v = jnp.split(u, 2, axis=-1)
    return (jax.nn.silu(g) * v) @ w_down


def setup_inputs(seed: int = 0) -> dict:
    key = jax.random.key(seed)
    ks = jax.random.split(key, 26)
    f32 = jnp.float32

    def nrm(k, shape, scale):
        return scale * jax.random.normal(k, shape, f32)

    def gain(k, n):
        return 1.0 + 0.05 * jax.random.normal(k, (DEPTH, n), f32)

    dt0 = jnp.exp(jax.random.uniform(ks[17], (DEPTH, SSM_HEADS), f32, math.log(1e-3), math.log(1e-1)))
    return {
        'x': nrm(ks[0], (BATCH, SEQ, D_MODEL), 1.0),
        'meta_tokens': nrm(ks[1], (N_META, D_MODEL), 1.0),
        'norm_mix_pre': gain(ks[2], D_MODEL),
        'norm_mix_post': gain(ks[3], D_MODEL),
        'norm_ffn_pre': gain(ks[4], D_MODEL),
        'norm_ffn_post': gain(ks[5], D_MODEL),
        'w_in': nrm(ks[6], (DEPTH, D_MODEL, D_IN), D_MODEL ** -0.5),
        'q_a_norm': gain(ks[7], Q_LORA_RANK),
        'w_uq': nrm(ks[8], (DEPTH, Q_LORA_RANK, MLA_HEADS, QK_NOPE_DIM + QK_ROPE_DIM), Q_LORA_RANK ** -0.5),
        'kv_a_norm': gain(ks[9], KV_LORA_RANK),
        'w_ukv': nrm(ks[10], (DEPTH, KV_LORA_RANK, MLA_HEADS, QK_NOPE_DIM + V_HEAD_DIM), KV_LORA_RANK ** -0.5),
        'attn_out_norm': gain(ks[11], D_ATTN),
        'ssm_conv_w': nrm(ks[12], (DEPTH, SSM_CONV, D_XBC), SSM_CONV ** -0.5),
        'ssm_conv_b': nrm(ks[13], (DEPTH, D_XBC), 0.02),
        'ssm_dt_bias': dt0 + jnp.log(-jnp.expm1(-dt0)),
        'ssm_A_log': jnp.log(jax.random.uniform(ks[14], (DEPTH, SSM_HEADS), f32, 1.0, 16.0)),
        'ssm_D': 1.0 + 0.1 * jax.random.normal(ks[15], (DEPTH, SSM_HEADS), f32),
        'ssm_norm': gain(ks[16], D_SSM),
        'w_out': nrm(ks[18], (DEPTH, D_MIX, D_MODEL), D_MIX ** -0.5),
        'w_up': nrm(ks[19], (DEPTH, D_MODEL, 2 * D_FF), D_MODEL ** -0.5),
        'ffn_conv_w': nrm(ks[20], (DEPTH, FFN_CONV, 2 * D_FF), FFN_CONV ** -0.5),
        'ffn_conv_b': nrm(ks[21], (DEPTH, 2 * D_FF), 0.02),
        'w_down': nrm(ks[22], (DEPTH, D_FF, D_MODEL), D_FF ** -0.5),
    }


def reference(x, meta_tokens, norm_mix_pre, norm_mix_post, norm_ffn_pre, norm_ffn_post, w_in,
              q_a_norm, w_uq, kv_a_norm, w_ukv, attn_out_norm, ssm_conv_w, ssm_conv_b, ssm_dt_bias,
              ssm_A_log, ssm_D, ssm_norm, w_out, w_up, ffn_conv_w, ffn_conv_b, w_down):
    bsz = x.shape[0]
    L = N_META + x.shape[1]
    meta = jnp.broadcast_to(meta_tokens[None].astype(x.dtype), (bsz, N_META, D_MODEL))
    h = jnp.concatenate([meta, x], axis=1)
    cos, sin = rope_tables(L)
    for l in range(DEPTH):
        hn = rms_norm(h, norm_mix_pre[l])
        q_c, kv_c, k_pe, z, xbc, dt_raw = split_cols(hn @ w_in[l], IN_SPLITS)
        attn = mla_group(q_c, kv_c, k_pe, q_a_norm[l], w_uq[l], kv_a_norm[l], w_ukv[l], cos, sin)
        ssm = ssd_group(z, xbc, dt_raw, ssm_conv_w[l], ssm_conv_b[l], ssm_dt_bias[l], ssm_A_log[l],
                        ssm_D[l], ssm_norm[l])
        mix = jnp.concatenate([rms_norm(attn, attn_out_norm[l]), ssm], axis=-1) @ w_out[l]
        h = h + rms_norm(mix, norm_mix_post[l])
        hn = rms_norm(h, norm_ffn_pre[l])
        h = h + rms_norm(conv_ffn(hn, w_up[l], ffn_conv_w[l], ffn_conv_b[l], w_down[l]), norm_ffn_post[l])
    return h[:, N_META:]
```

```python
import functools
import math

import jax
import jax.numpy as jnp
import numpy as np
from jax import lax
from jax.experimental import pallas as pl
from jax.experimental.pallas import tpu as pltpu

F32 = jnp.float32
BF16 = jnp.bfloat16

D_MODEL = 1024
N_META = 16
MLA_HEADS = 8
QK_NOPE = 128
QK_ROPE = 64
V_DIM = 128
Q_RANK = 384
KV_RANK = 256
ROPE_THETA = 10000.0
SOFTMAX_SCALE = (QK_NOPE + QK_ROPE) ** -0.5
D_ATTN = MLA_HEADS * V_DIM
SSM_HEADS = 16
SSM_P = 64
SSM_GROUPS = 2
SSM_N = 128
SSM_CONV = 4
CHUNK = 128
D_SSM = SSM_HEADS * SSM_P
D_XBC = D_SSM + 2 * SSM_GROUPS * SSM_N
D_FF = 2816
FFN_CONV = 3
EPS = 1e-6

LANES = 128
SUBLANES = 8
ROW_TILE = 512
BLK = 128
PAD_ROWS = BLK - N_META
QK_PAD = 256
FF_CHUNK = 256
N_FF_CHUNKS = D_FF // FF_CHUNK
LAT_W = 768
IN_W = LAT_W + D_SSM + D_XBC + LANES
NEG = -1e30
VMEM_LIMIT = 56 * 1024 * 1024


def _dot(a, b):
    return jnp.dot(a, b, preferred_element_type=F32)


def _dot_nt(a, b):
    return lax.dot_general(a, b, (((1,), (1,)), ((), ())), preferred_element_type=F32)


def _rms(x, gain):
    ms = jnp.mean(x * x, axis=-1, keepdims=True)
    return x * lax.rsqrt(ms + EPS) * gain


def _silu(x):
    return x * (1.0 / (1.0 + jnp.exp(-x)))


def _const_spec(shape):
    nd = len(shape)
    return pl.BlockSpec(shape, lambda *_: (0,) * nd, pipeline_mode=pl.Buffered(1))


def _params():
    return pltpu.CompilerParams(dimension_semantics=("arbitrary", "arbitrary"),
                                vmem_limit_bytes=VMEM_LIMIT)


def _front_kernel(n_real, x_ref, meta_ref, g_ref, win_ref, qan_ref, wqn_ref, wqp_ref, wqs_ref,
                  kvan_ref, wkv_ref, qc_ref, qs_ref, kt_ref,
                  q_ref, k_ref, vt_ref, z_ref, xbc_ref, dt_ref):
    i = pl.program_id(1)
    h = jnp.where(i == n_real, meta_ref[...], x_ref[0])
    hn = _rms(h, g_ref[...]).astype(BF16)
    lat = _dot(hn, win_ref[:, 0:LAT_W])
    o = LAT_W
    z_ref[0] = _dot(hn, win_ref[:, o:o + D_SSM])
    o += D_SSM
    xbc_ref[0] = _dot(hn, win_ref[:, o:o + D_XBC])
    o += D_XBC
    dt_ref[0] = _dot(hn, win_ref[:, o:o + LANES])

    qn = _rms(lat[:, 0:Q_RANK], qan_ref[...]).astype(BF16)
    q_nope = _dot(qn, wqn_ref[...])
    q_pe = _dot(qn, wqp_ref[...])
    q_sw = _dot(qn, wqs_ref[...])
    qc = qc_ref[...]
    qs = qs_ref[...]
    kvn = _rms(lat[:, Q_RANK:Q_RANK + KV_RANK], kvan_ref[...]).astype(BF16)
    kv = _dot(kvn, wkv_ref[...])
    a = lat[:, Q_RANK + KV_RANK:LAT_W] * kt_ref[...]
    rot = a + pltpu.roll(a, QK_ROPE, 1)
    lane = lax.broadcasted_iota(jnp.int32, rot.shape, 1)
    k_pe = jnp.where(lane < QK_ROPE, rot, 0.0).astype(BF16)
    for hd in range(MLA_HEADS):
        sl = slice(hd * LANES, (hd + 1) * LANES)
        q_ref[0, hd, :, 0:LANES] = q_nope[:, sl].astype(BF16)
        q_ref[0, hd, :, LANES:QK_PAD] = (q_pe[:, sl] * qc + q_sw[:, sl] * qs).astype(BF16)
        k_ref[0, hd, :, 0:LANES] = kv[:, sl].astype(BF16)
        k_ref[0, hd, :, LANES:QK_PAD] = k_pe
        v = kv[:, D_ATTN + hd * LANES:D_ATTN + (hd + 1) * LANES]
        vt_ref[0, hd, 0] = v.T.astype(BF16)


def _front_call(x, meta_tile, gain, w_in, q_an, wq_nope, wq_pe, wq_sw, kv_an, w_kv, q_cos, q_sin, k_tab):
    bsz, seq, _ = x.shape
    n_real = seq // ROW_TILE
    n_tiles = n_real + 1
    lp = n_tiles * ROW_TILE
    row = lambda w: pl.BlockSpec((1, ROW_TILE, w), lambda b, i: (b, i, 0))
    tab = pl.BlockSpec((ROW_TILE, LANES), lambda b, i: (i, 0))
    qk_spec = pl.BlockSpec((1, MLA_HEADS, ROW_TILE, QK_PAD), lambda b, i: (b, 0, i, 0))
    return pl.pallas_call(
        functools.partial(_front_kernel, n_real),
        grid=(bsz, n_tiles),
        in_specs=[
            pl.BlockSpec((1, ROW_TILE, D_MODEL), lambda b, i: (b, jnp.minimum(i, n_real - 1), 0)),
            _const_spec((ROW_TILE, D_MODEL)),
            _const_spec((1, D_MODEL)),
            _const_spec((D_MODEL, IN_W)),
            _const_spec((1, Q_RANK)),
            _const_spec((Q_RANK, D_ATTN)),
            _const_spec((Q_RANK, D_ATTN)),
            _const_spec((Q_RANK, D_ATTN)),
            _const_spec((1, KV_RANK)),
            _const_spec((KV_RANK, 2 * D_ATTN)),
            tab, tab, tab,
        ],
        out_specs=[
            qk_spec, qk_spec,
            pl.BlockSpec((1, MLA_HEADS, 1, V_DIM, ROW_TILE), lambda b, i: (b, 0, i, 0, 0)),
            row(D_SSM), row(D_XBC), row(LANES),
        ],
        out_shape=[
            jax.ShapeDtypeStruct((bsz, MLA_HEADS, lp, QK_PAD), BF16),
            jax.ShapeDtypeStruct((bsz, MLA_HEADS, lp, QK_PAD), BF16),
            jax.ShapeDtypeStruct((bsz, MLA_HEADS, n_tiles, V_DIM, ROW_TILE), BF16),
            jax.ShapeDtypeStruct((bsz, lp, D_SSM), F32),
            jax.ShapeDtypeStruct((bsz, lp, D_XBC), F32),
            jax.ShapeDtypeStruct((bsz, lp, LANES), F32),
        ],
        compiler_params=_params(),
        name="front",
    )(x, meta_tile, gain, w_in, q_an, wq_nope, wq_pe, wq_sw, kv_an, w_kv, q_cos, q_sin, k_tab)


def _attn_kernel(n_real, q_ref, k_ref, vt_ref, o_ref, m_ref, l_ref, acc_ref, mp_ref, lp_ref, accp_ref):
    pre = n_real * ROW_TILE

    def step(q, k_rows, vt, mask, m_r, l_r, acc_r):
        s = _dot_nt(k_rows, q)
        if mask is not None:
            s = jnp.where(mask, s, NEG)
        m_prev = m_r[...]
        m_new = jnp.maximum(m_prev, jnp.max(s, axis=0, keepdims=True))
        alpha = jnp.exp(m_prev - m_new)
        p = jnp.exp(s - m_new)
        l_r[...] = alpha * l_r[...] + jnp.sum(p, axis=0, keepdims=True)
        acc_r[...] = alpha * acc_r[...] + _dot(vt, p.astype(BF16))
        m_r[...] = m_new

    def init(m_r, l_r, acc_r):
        m_r[...] = jnp.full(m_r.shape, NEG, F32)
        l_r[...] = jnp.zeros(l_r.shape, F32)
        acc_r[...] = jnp.zeros(acc_r.shape, F32)

    k_pre = k_ref[0, 0, pre:pre + BLK, :]
    vt_pre = vt_ref[0, 0, n_real, :, 0:BLK]

    kk = lax.broadcasted_iota(jnp.int32, (BLK, BLK), 0)
    qq = lax.broadcasted_iota(jnp.int32, (BLK, BLK), 1)
    init(mp_ref, lp_ref, accp_ref)
    step(q_ref[0, 0, pre:pre + BLK, :], k_pre, vt_pre, (kk >= PAD_ROWS) & (kk <= qq), mp_ref, lp_ref, accp_ref)
    o_ref[0, pre:pre + BLK, :] = (accp_ref[...] / lp_ref[...]).T
    o_ref[0, pre + BLK:, :] = jnp.zeros((ROW_TILE - BLK, V_DIM), F32)

    kpad_ok = lax.broadcasted_iota(jnp.int32, (BLK, ROW_TILE), 0) >= PAD_ROWS
    kd = lax.broadcasted_iota(jnp.int32, (ROW_TILE, ROW_TILE), 0)
    qd = lax.broadcasted_iota(jnp.int32, (ROW_TILE, ROW_TILE), 1)
    diag_ok = kd <= qd

    def q_tile(qi, carry):
        qs = pl.multiple_of(qi * ROW_TILE, ROW_TILE)
        q = q_ref[0, 0, pl.ds(qs, ROW_TILE), :]
        init(m_ref, l_ref, acc_ref)
        step(q, k_pre, vt_pre, kpad_ok, m_ref, l_ref, acc_ref)

        def kv_tile(j, c):
            ks = pl.multiple_of(j * ROW_TILE, ROW_TILE)
            step(q, k_ref[0, 0, pl.ds(ks, ROW_TILE), :], vt_ref[0, 0, j], None, m_ref, l_ref, acc_ref)
            return c

        lax.fori_loop(0, qi, kv_tile, 0)
        step(q, k_ref[0, 0, pl.ds(qs, ROW_TILE), :], vt_ref[0, 0, qi], diag_ok, m_ref, l_ref, acc_ref)
        o_ref[0, pl.ds(qs, ROW_TILE), :] = (acc_ref[...] / l_ref[...]).T
        return carry

    lax.fori_loop(0, n_real, q_tile, 0)


def _attn_call(q, k, vt):
    bsz, _, lp, _ = q.shape
    n_real = lp // ROW_TILE - 1
    qk_spec = pl.BlockSpec((1, 1, lp, QK_PAD), lambda b, h: (b, h, 0, 0))
    return pl.pallas_call(
        functools.partial(_attn_kernel, n_real),
        grid=(bsz, MLA_HEADS),
        in_specs=[qk_spec, qk_spec,
                  pl.BlockSpec((1, 1, n_real + 1, V_DIM, ROW_TILE), lambda b, h: (b, h, 0, 0, 0))],
        out_specs=pl.BlockSpec((1, lp, V_DIM), lambda b, h: (b, 0, h)),
        out_shape=jax.ShapeDtypeStruct((bsz, lp, D_ATTN), F32),
        scratch_shapes=[
            pltpu.VMEM((1, ROW_TILE), F32), pltpu.VMEM((1, ROW_TILE), F32), pltpu.VMEM((V_DIM, ROW_TILE), F32),
            pltpu.VMEM((1, BLK), F32), pltpu.VMEM((1, BLK), F32), pltpu.VMEM((V_DIM, BLK), F32),
        ],
        compiler_params=_params(),
        name="attn",
    )(q, k, vt)


N_PAIRS = SSM_HEADS // 2


def _split3(a):
    a1 = a.astype(BF16)
    r1 = a - a1.astype(F32)
    a2 = r1.astype(BF16)
    a3 = (r1 - a2.astype(F32)).astype(BF16)
    return a1, a2, a3


def _ssd_kernel(n_chunks, z_ref, xbc_ref, dt_ref, cw_ref, cb_ref, dtb_ref, alog_ref, dskip_ref, ng_ref,
                y_ref, xbuf_ref, state_ref):
    c = pl.program_id(1)

    @pl.when(c == 0)
    def _():
        xbuf_ref[0:SUBLANES, :] = jnp.zeros((SUBLANES, D_XBC), F32)
        state_ref[...] = jnp.zeros(state_ref.shape, F32)

    @pl.when(c >= n_chunks)
    def _():
        y_ref[0] = jnp.zeros((CHUNK, D_SSM), F32)

    @pl.when(c < n_chunks)
    def _():
        x = xbc_ref[0]
        xbuf_ref[SUBLANES:SUBLANES + CHUNK, :] = x
        conv = cb_ref[...] + cw_ref[SSM_CONV - 1:SSM_CONV, :] * x
        for d in range(1, SSM_CONV):
            conv = conv + cw_ref[SSM_CONV - 1 - d:SSM_CONV - d, :] * xbuf_ref[SUBLANES - d:SUBLANES - d + CHUNK, :]
        xbuf_ref[0:SUBLANES, :] = x[CHUNK - SUBLANES:CHUNK, :]
        xc = _silu(conv)
        xs = xc[:, 0:D_SSM]

        row = lax.broadcasted_iota(jnp.int32, (CHUNK, LANES), 0)
        col = lax.broadcasted_iota(jnp.int32, (CHUNK, LANES), 1)
        dt = jax.nn.softplus(dt_ref[0] + dtb_ref[...])
        dt = jnp.where((c > 0) | (row >= PAD_ROWS), dt, 0.0)
        a = dt * (-jnp.exp(alog_ref[...]))
        tri = (col <= row).astype(BF16)
        a1, a2, a3 = _split3(a)
        acs = _dot(tri, a1) + _dot(tri, a2) + _dot(tri, a3)
        acs_t = acs.T
        dt_t = dt.T
        e_acs = jnp.exp(acs)
        last_t = acs_t[:, CHUNK - 1:CHUNK]
        w_t = dt_t * jnp.exp(last_t - acs_t)
        dec_t = jnp.exp(last_t)
        causal = col <= row
        left = col < SSM_P

        for g in range(SSM_GROUPS):
            bm = xc[:, D_SSM + g * SSM_N:D_SSM + (g + 1) * SSM_N]
            cm = xc[:, D_SSM + SSM_GROUPS * SSM_N + g * SSM_N:D_SSM + SSM_GROUPS * SSM_N + (g + 1) * SSM_N]
            bm16 = bm.astype(BF16)
            cm16 = cm.astype(BF16)
            cb = _dot_nt(cm16, bm16)
            bm_t = bm.T
            for pp in range(N_PAIRS // SSM_GROUPS):
                pair = g * (N_PAIRS // SSM_GROUPS) + pp
                xs_pair = xs[:, pair * LANES:(pair + 1) * LANES]
                prev = state_ref[pair]
                rhs_x = [jnp.where(left, xs_pair, 0.0).astype(BF16), jnp.where(left, 0.0, xs_pair).astype(BF16)]
                rhs_s = [jnp.where(left, prev, 0.0).astype(BF16), jnp.where(left, 0.0, prev).astype(BF16)]
                lhs_m, lhs_c, lhs_b, decs = [], [], [], []
                for k in range(2):
                    hd = 2 * pair + k
                    seg = acs[:, hd:hd + 1] - acs_t[hd:hd + 1, :]
                    lm = jnp.exp(jnp.where(causal, seg, NEG))
                    lhs_m.append((cb * lm * dt_t[hd:hd + 1, :]).astype(BF16))
                    lhs_c.append((cm * e_acs[:, hd:hd + 1]).astype(BF16))
                    lhs_b.append((bm_t * w_t[hd:hd + 1, :]).astype(BF16))
                    decs.append(jnp.broadcast_to(dec_t[hd:hd + 1, :], (SSM_N, LANES)))
                y_pair = _dot(jnp.concatenate(lhs_m + lhs_c, axis=1), jnp.concatenate(rhs_x + rhs_s, axis=0))
                new = _dot(jnp.concatenate(lhs_b, axis=1), jnp.concatenate(rhs_x, axis=0))
                state_ref[pair] = jnp.where(left, decs[0], decs[1]) * prev + new
                y_ref[0, :, pair * LANES:(pair + 1) * LANES] = y_pair

        y = (y_ref[0] + dskip_ref[...] * xs) * _silu(z_ref[0])
        gsz = D_SSM // SSM_GROUPS
        for g in range(SSM_GROUPS):
            yg = y[:, g * gsz:(g + 1) * gsz]
            ms = jnp.mean(yg * yg, axis=-1, keepdims=True)
            y_ref[0, :, g * gsz:(g + 1) * gsz] = yg * lax.rsqrt(ms + EPS) * ng_ref[:, g * gsz:(g + 1) * gsz]


def _ssd_call(z, xbc, dt, conv_w, conv_b, dt_bias, a_log, d_skip, norm_gain):
    bsz, lp, _ = z.shape
    n_blocks = lp // CHUNK
    pre_blk = (lp - ROW_TILE) // CHUNK
    n_chunks = pre_blk + 1

    def blk(b, c):
        return (b, jnp.where(c == 0, pre_blk, jnp.where(c < n_chunks, c - 1, c)), 0)

    row = lambda w: pl.BlockSpec((1, CHUNK, w), blk)
    return pl.pallas_call(
        functools.partial(_ssd_kernel, n_chunks),
        grid=(bsz, n_blocks),
        in_specs=[row(D_SSM), row(D_XBC), row(LANES),
                  _const_spec((SSM_CONV, D_XBC)), _const_spec((1, D_XBC)), _const_spec((1, LANES)),
                  _const_spec((1, LANES)), _const_spec((1, D_SSM)), _const_spec((1, D_SSM))],
        out_specs=row(D_SSM),
        out_shape=jax.ShapeDtypeStruct((bsz, lp, D_SSM), F32),
        scratch_shapes=[pltpu.VMEM((SUBLANES + CHUNK, D_XBC), F32), pltpu.VMEM((N_PAIRS, SSM_N, LANES), F32)],
        compiler_params=_params(),
        name="ssd",
    )(z, xbc, dt, conv_w, conv_b, dt_bias, a_log, d_skip, norm_gain)


def _mix_kernel(n_real, x_ref, meta_ref, attn_ref, ssm_ref, ga_ref, wout_ref, gp_ref, h1_ref):
    i = pl.program_id(1)
    h = jnp.where(i == n_real, meta_ref[...], x_ref[0])
    an = _rms(attn_ref[0], ga_ref[...]).astype(BF16)
    mix = _dot(an, wout_ref[0:D_ATTN, :]) + _dot(ssm_ref[0].astype(BF16), wout_ref[D_ATTN:, :])
    h1_ref[0] = h + _rms(mix, gp_ref[...])


def _mix_call(x, meta_tile, attn, ssm, gain_attn, w_out, gain_post):
    bsz, lp, _ = attn.shape
    n_real = lp // ROW_TILE - 1
    row = pl.BlockSpec((1, ROW_TILE, D_MODEL), lambda b, i: (b, i, 0))
    return pl.pallas_call(
        functools.partial(_mix_kernel, n_real),
        grid=(bsz, n_real + 1),
        in_specs=[pl.BlockSpec((1, ROW_TILE, D_MODEL), lambda b, i: (b, jnp.minimum(i, n_real - 1), 0)),
                  _const_spec((ROW_TILE, D_MODEL)), row, row,
                  _const_spec((1, D_ATTN)), _const_spec((D_ATTN + D_SSM, D_MODEL)), _const_spec((1, D_MODEL))],
        out_specs=row,
        out_shape=jax.ShapeDtypeStruct((bsz, lp, D_MODEL), F32),
        compiler_params=_params(),
        name="mix",
    )(x, meta_tile, attn, ssm, gain_attn, w_out, gain_post)


def _shifted(u, tail, d):
    r = pltpu.roll(u, d, 0)
    row = lax.broadcasted_iota(jnp.int32, (SUBLANES, u.shape[1]), 0)
    head = jnp.where(row < d, pltpu.roll(tail, d, 0), r[0:SUBLANES])
    return jnp.concatenate([head, r[SUBLANES:]], axis=0)


def _ffn_kernel(h1_ref, h1m_ref, gpre_ref, wg_ref, wv_ref, cwg_ref, cwv_ref, wd_ref, gpost_ref,
                out_ref, tg_ref, tv_ref, acc_ref):
    i = pl.program_id(1)
    g_pre = gpre_ref[...]

    @pl.when(i == 0)
    def _():
        hm = _rms(h1m_ref[0, BLK - SUBLANES:BLK, :], g_pre).astype(BF16)

        def seed(c, carry):
            tg_ref[c] = _dot(hm, wg_ref[c])
            tv_ref[c] = _dot(hm, wv_ref[c])
            return carry

        lax.fori_loop(0, N_FF_CHUNKS, seed, 0)

    h1 = h1_ref[0]
    hn = _rms(h1, g_pre).astype(BF16)
    acc_ref[...] = jnp.zeros(acc_ref.shape, F32)

    def conv(u, tail, cw):
        return (cw[FFN_CONV:FFN_CONV + 1, :] + cw[2:3, :] * u + cw[1:2, :] * _shifted(u, tail, 1)
                + cw[0:1, :] * _shifted(u, tail, 2))

    def chunk(c, carry):
        ug = _dot(hn, wg_ref[c])
        uv = _dot(hn, wv_ref[c])
        cg = conv(ug, tg_ref[c], cwg_ref[c])
        cv = conv(uv, tv_ref[c], cwv_ref[c])
        tg_ref[c] = ug[ROW_TILE - SUBLANES:ROW_TILE, :]
        tv_ref[c] = uv[ROW_TILE - SUBLANES:ROW_TILE, :]
        act = (_silu(cg) * cv).astype(BF16)
        acc_ref[...] += _dot(act, wd_ref[c])
        return carry

    lax.fori_loop(0, N_FF_CHUNKS, chunk, 0)
    out_ref[0] = h1 + _rms(acc_ref[...], gpost_ref[...])


def _ffn_call(h1, gain_pre, w_gate, w_val, cw_gate, cw_val, w_down, gain_post):
    bsz, lp, _ = h1.shape
    seq = lp - ROW_TILE
    return pl.pallas_call(
        _ffn_kernel,
        grid=(bsz, seq // ROW_TILE),
        in_specs=[pl.BlockSpec((1, ROW_TILE, D_MODEL), lambda b, i: (b, i, 0)),
                  pl.BlockSpec((1, BLK, D_MODEL), lambda b, i: (b, seq // BLK, 0)),
                  _const_spec((1, D_MODEL)),
                  _const_spec((N_FF_CHUNKS, D_MODEL, FF_CHUNK)), _const_spec((N_FF_CHUNKS, D_MODEL, FF_CHUNK)),
                  _const_spec((N_FF_CHUNKS, SUBLANES, FF_CHUNK)), _const_spec((N_FF_CHUNKS, SUBLANES, FF_CHUNK)),
                  _const_spec((N_FF_CHUNKS, FF_CHUNK, D_MODEL)), _const_spec((1, D_MODEL))],
        out_specs=pl.BlockSpec((1, ROW_TILE, D_MODEL), lambda b, i: (b, i, 0)),
        out_shape=jax.ShapeDtypeStruct((bsz, seq, D_MODEL), F32),
        scratch_shapes=[pltpu.VMEM((N_FF_CHUNKS, SUBLANES, FF_CHUNK), F32),
                        pltpu.VMEM((N_FF_CHUNKS, SUBLANES, FF_CHUNK), F32),
                        pltpu.VMEM((ROW_TILE, D_MODEL), F32)],
        compiler_params=_params(),
        name="ffn",
    )(h1, h1, gain_pre, w_gate, w_val, cw_gate, cw_val, w_down, gain_post)


def _swap_halves(t):
    half = t.shape[-1] // 2
    return jnp.concatenate([t[..., half:], t[..., :half]], axis=-1)


def _rope_tables(seq, lp):
    n = N_META + seq
    inv = ROPE_THETA ** (-jnp.arange(0, QK_ROPE, 2, dtype=F32) / QK_ROPE)
    ang = jnp.arange(n, dtype=F32)[:, None] * inv[None, :]
    cos, sin = jnp.cos(ang), jnp.sin(ang)

    def place(t):
        out = jnp.zeros((lp, t.shape[1]), F32)
        out = out.at[0:seq].set(t[N_META:])
        return out.at[seq + PAD_ROWS:seq + BLK].set(t[:N_META])

    zeros = jnp.zeros((n, QK_ROPE), F32)
    q_cos = place(jnp.concatenate([cos, cos, zeros], axis=1))
    q_sin = place(jnp.concatenate([-sin, sin, zeros], axis=1))
    k_tab = place(jnp.concatenate([cos, cos, -sin, sin], axis=1))
    return q_cos, q_sin, k_tab


def _pad_lanes(v, width=LANES):
    return jnp.zeros((1, width), F32).at[0, :v.shape[0]].set(v)


def kernel(x, meta_tokens, norm_mix_pre, norm_mix_post, norm_ffn_pre, norm_ffn_post, w_in, q_a_norm, w_uq,
           kv_a_norm, w_ukv, attn_out_norm, ssm_conv_w, ssm_conv_b, ssm_dt_bias, ssm_A_log, ssm_D, ssm_norm,
           w_out, w_up, ffn_conv_w, ffn_conv_b, w_down):
    bsz, seq, _ = x.shape
    lp = seq + ROW_TILE
    l = 0
    meta_tile = jnp.zeros((ROW_TILE, D_MODEL), F32).at[PAD_ROWS:BLK].set(meta_tokens.astype(F32))

    o_kv, o_pe, o_z = Q_RANK, Q_RANK + KV_RANK, Q_RANK + KV_RANK + QK_ROPE
    o_xbc, o_dt = o_z + D_SSM, o_z + D_SSM + D_XBC
    wi = w_in[l]
    w_pe = wi[:, o_pe:o_z]
    w_in_p = jnp.concatenate(
        [wi[:, :o_pe], w_pe, _swap_halves(w_pe), wi[:, o_z:o_dt],
         wi[:, o_dt:], jnp.zeros((D_MODEL, LANES - SSM_HEADS), F32)], axis=1).astype(BF16)

    wq = w_uq[l] * SOFTMAX_SCALE
    wq_nope = wq[:, :, :QK_NOPE].reshape(Q_RANK, D_ATTN).astype(BF16)
    pad = jnp.zeros((Q_RANK, MLA_HEADS, LANES - QK_ROPE), F32)
    wq_pe = jnp.concatenate([wq[:, :, QK_NOPE:], pad], axis=2).reshape(Q_RANK, D_ATTN).astype(BF16)
    wq_sw = jnp.concatenate([_swap_halves(wq[:, :, QK_NOPE:]), pad], axis=2).reshape(Q_RANK, D_ATTN).astype(BF16)
    wkv = w_ukv[l]
    w_kv = jnp.concatenate([wkv[:, :, :QK_NOPE].reshape(KV_RANK, D_ATTN),
                            wkv[:, :, QK_NOPE:].reshape(KV_RANK, D_ATTN)], axis=1).astype(BF16)
    q_cos, q_sin, k_tab = _rope_tables(seq, lp)

    q, k, vt, z, xbc, dt = _front_call(
        x, meta_tile, norm_mix_pre[l][None], w_in_p, q_a_norm[l][None], wq_nope, wq_pe, wq_sw,
        kv_a_norm[l][None], w_kv, q_cos, q_sin, k_tab)

    attn = _attn_call(q, k, vt)

    ssm = _ssd_call(z, xbc, dt, ssm_conv_w[l], ssm_conv_b[l][None], _pad_lanes(ssm_dt_bias[l]),
                    _pad_lanes(ssm_A_log[l]), jnp.repeat(ssm_D[l], SSM_P)[None], ssm_norm[l][None])

    h1 = _mix_call(x, meta_tile, attn, ssm, attn_out_norm[l][None], w_out[l].astype(BF16),
                   norm_mix_post[l][None])

    def chunks_cols(w):
        return jnp.transpose(w.reshape(w.shape[0], N_FF_CHUNKS, FF_CHUNK), (1, 0, 2))

    wu = w_up[l]
    cw = jnp.concatenate([ffn_conv_w[l], ffn_conv_b[l][None],
                          jnp.zeros((SUBLANES - FFN_CONV - 1, 2 * D_FF), F32)], axis=0)
    return _ffn_call(
        h1, norm_ffn_pre[l][None],
        chunks_cols(wu[:, :D_FF]).astype(BF16), chunks_cols(wu[:, D_FF:]).astype(BF16),
        chunks_cols(cw[:, :D_FF]), chunks_cols(cw[:, D_FF:]),
        w_down[l].reshape(N_FF_CHUNKS, FF_CHUNK, D_MODEL).astype(BF16), norm_ffn_post[l][None])
```

```python
import functools
import math

import jax
import jax.numpy as jnp
import numpy as np
from jax import lax
from jax.experimental import pallas as pl
from jax.experimental.pallas import tpu as pltpu

F32 = jnp.float32
BF16 = jnp.bfloat16

D_MODEL = 1024
N_META = 16
MLA_HEADS = 8
QK_NOPE = 128
QK_ROPE = 64
V_DIM = 128
Q_RANK = 384
KV_RANK = 256
ROPE_THETA = 10000.0
SOFTMAX_SCALE = (QK_NOPE + QK_ROPE) ** -0.5
D_ATTN = MLA_HEADS * V_DIM
SSM_HEADS = 16
SSM_P = 64
SSM_GROUPS = 2
SSM_N = 128
SSM_CONV = 4
CHUNK = 128
D_SSM = SSM_HEADS * SSM_P
D_XBC = D_SSM + 2 * SSM_GROUPS * SSM_N
D_FF = 2816
FFN_CONV = 3
EPS = 1e-6

LANES = 128
SUBLANES = 8
ROW_TILE = 512
BLK = 128
QK_PAD = 256
FF_CHUNK = 256
ATTN_UNROLL = 4
N_FF_CHUNKS = D_FF // FF_CHUNK
LAT_W = 768
IN_W = LAT_W + D_SSM + D_XBC + LANES
NEG = -1e30
VMEM_LIMIT = 56 * 1024 * 1024


def _dot(a, b):
    return jnp.dot(a, b, preferred_element_type=F32)


def _dot_nt(a, b):
    return lax.dot_general(a, b, (((1,), (1,)), ((), ())), preferred_element_type=F32)


def _rms(x, gain):
    ms = jnp.mean(x * x, axis=-1, keepdims=True)
    return x * lax.rsqrt(ms + EPS) * gain


def _silu(x):
    return x * (1.0 / (1.0 + jnp.exp(-x)))


def _const_spec(shape):
    nd = len(shape)
    return pl.BlockSpec(shape, lambda *_: (0,) * nd, pipeline_mode=pl.Buffered(1))


def _params():
    return pltpu.CompilerParams(dimension_semantics=("arbitrary", "arbitrary"),
                                vmem_limit_bytes=VMEM_LIMIT)


def _front_kernel(n_real, x_ref, meta_ref, g_ref, win_ref, qan_ref, wqn_ref, wqp_ref, wqs_ref,
                  kvan_ref, wkv_ref, qc_ref, qs_ref, kt_ref,
                  q_ref, k_ref, vt_ref, z_ref, xbc_ref, dt_ref):
    i = pl.program_id(1)
    h = jnp.where(i == n_real, meta_ref[...], x_ref[0])
    hn = _rms(h, g_ref[...]).astype(BF16)
    lat = _dot(hn, win_ref[:, 0:LAT_W])
    o = LAT_W
    z_ref[0] = _dot(hn, win_ref[:, o:o + D_SSM])
    o += D_SSM
    xbc_ref[0] = _dot(hn, win_ref[:, o:o + D_XBC])
    o += D_XBC
    dt_ref[0] = _dot(hn, win_ref[:, o:o + LANES])

    qn = _rms(lat[:, 0:Q_RANK], qan_ref[...]).astype(BF16)
    q_nope = _dot(qn, wqn_ref[...])
    q_pe = _dot(qn, wqp_ref[...])
    q_sw = _dot(qn, wqs_ref[...])
    qc = qc_ref[...]
    qs = qs_ref[...]
    kvn = _rms(lat[:, Q_RANK:Q_RANK + KV_RANK], kvan_ref[...]).astype(BF16)
    kv = _dot(kvn, wkv_ref[...])
    a = lat[:, Q_RANK + KV_RANK:LAT_W] * kt_ref[...]
    rot = a + pltpu.roll(a, QK_ROPE, 1)
    lane = lax.broadcasted_iota(jnp.int32, rot.shape, 1)
    k_pe = jnp.where(lane < QK_ROPE, rot, 0.0).astype(BF16)
    for hd in range(MLA_HEADS):
        sl = slice(hd * LANES, (hd + 1) * LANES)
        q_ref[0, hd, :, 0:LANES] = q_nope[:, sl].astype(BF16)
        q_ref[0, hd, :, LANES:QK_PAD] = (q_pe[:, sl] * qc + q_sw[:, sl] * qs).astype(BF16)
        k_ref[0, hd, :, 0:LANES] = kv[:, sl].astype(BF16)
        k_ref[0, hd, :, LANES:QK_PAD] = k_pe
        v = kv[:, D_ATTN + hd * LANES:D_ATTN + (hd + 1) * LANES]
        vt_ref[0, hd, 0] = v.T.astype(BF16)


def _front_call(x, meta_tile, gain, w_in, q_an, wq_nope, wq_pe, wq_sw, kv_an, w_kv, q_cos, q_sin, k_tab):
    bsz, seq, _ = x.shape
    n_real = seq // ROW_TILE
    n_tiles = n_real + 1
    lp = n_tiles * ROW_TILE
    row = lambda w: pl.BlockSpec((1, ROW_TILE, w), lambda b, i: (b, i, 0))
    tab = pl.BlockSpec((ROW_TILE, LANES), lambda b, i: (i, 0))
    qk_spec = pl.BlockSpec((1, MLA_HEADS, ROW_TILE, QK_PAD), lambda b, i: (b, 0, i, 0))
    return pl.pallas_call(
        functools.partial(_front_kernel, n_real),
        grid=(bsz, n_tiles),
        in_specs=[
            pl.BlockSpec((1, ROW_TILE, D_MODEL), lambda b, i: (b, jnp.minimum(i, n_real - 1), 0)),
            _const_spec((ROW_TILE, D_MODEL)),
            _const_spec((1, D_MODEL)),
            _const_spec((D_MODEL, IN_W)),
            _const_spec((1, Q_RANK)),
            _const_spec((Q_RANK, D_ATTN)),
            _const_spec((Q_RANK, D_ATTN)),
            _const_spec((Q_RANK, D_ATTN)),
            _const_spec((1, KV_RANK)),
            _const_spec((KV_RANK, 2 * D_ATTN)),
            tab, tab, tab,
        ],
        out_specs=[
            qk_spec, qk_spec,
            pl.BlockSpec((1, MLA_HEADS, 1, V_DIM, ROW_TILE), lambda b, i: (b, 0, i, 0, 0)),
            row(D_SSM), row(D_XBC), row(LANES),
        ],
        out_shape=[
            jax.ShapeDtypeStruct((bsz, MLA_HEADS, lp, QK_PAD), BF16),
            jax.ShapeDtypeStruct((bsz, MLA_HEADS, lp, QK_PAD), BF16),
            jax.ShapeDtypeStruct((bsz, MLA_HEADS, n_tiles, V_DIM, ROW_TILE), BF16),
            jax.ShapeDtypeStruct((bsz, lp, D_SSM), F32),
            jax.ShapeDtypeStruct((bsz, lp, D_XBC), F32),
            jax.ShapeDtypeStruct((bsz, lp, LANES), F32),
        ],
        compiler_params=_params(),
        name="front",
    )(x, meta_tile, gain, w_in, q_an, wq_nope, wq_pe, wq_sw, kv_an, w_kv, q_cos, q_sin, k_tab)


def _attn_kernel(n_real, q_ref, k_ref, vt_ref, o_ref, s0_ref, s1_ref, c0_ref, c1_ref, m_ref, l_ref, acc_ref):
    pre = n_real * ROW_TILE
    s_refs, c_refs = (s0_ref, s1_ref), (c0_ref, c1_ref)
    k_meta = k_ref[0, 0, pre:pre + N_META, :]
    vt_meta = vt_ref[0, 0, n_real, :, 0:N_META]

    def softmax_cols(s):
        m = jnp.max(s, axis=0, keepdims=True)
        p = jnp.exp2(s - m)
        return m, jnp.sum(p, axis=0, keepdims=True), p

    s = _dot_nt(k_meta, q_ref[0, 0, pre:pre + BLK, :])
    kk = lax.broadcasted_iota(jnp.int32, s.shape, 0)
    qq = lax.broadcasted_iota(jnp.int32, s.shape, 1)
    _, l, p = softmax_cols(jnp.where(kk <= qq, s, NEG))
    o_ref[0, pre:pre + BLK, :] = (_dot(vt_meta, p.astype(BF16)) / l).T
    o_ref[0, pre + BLK:, :] = jnp.zeros((ROW_TILE - BLK, V_DIM), F32)

    def init_tile(qi, carry):
        qs = pl.multiple_of(qi * ROW_TILE, ROW_TILE)
        m, l, p = softmax_cols(_dot_nt(k_meta, q_ref[0, 0, pl.ds(qs, ROW_TILE), :]))
        m_ref[qi] = m
        l_ref[qi] = l
        acc_ref[qi] = _dot(vt_meta, p.astype(BF16))
        return carry

    lax.fori_loop(0, n_real, init_tile, 0)

    def scores(qi, j):
        qs = pl.multiple_of(qi * ROW_TILE, ROW_TILE)
        ks = pl.multiple_of(j * ROW_TILE, ROW_TILE)
        return _dot_nt(k_ref[0, 0, pl.ds(ks, ROW_TILE), :], q_ref[0, 0, pl.ds(qs, ROW_TILE), :])

    def run(n_steps, first, advance, diagonal):
        def produce(slot, qi, j):
            s = scores(jnp.minimum(qi, n_real - 1), jnp.minimum(j, n_real - 1))
            if diagonal:
                kd = lax.broadcasted_iota(jnp.int32, s.shape, 0)
                qd = lax.broadcasted_iota(jnp.int32, s.shape, 1)
                s = jnp.where(kd <= qd, s, NEG)
            s_refs[slot][...] = s
            c_refs[slot][...] = jnp.max(s, axis=0, keepdims=True)

        def step(slot, qi, j):
            nqi, nj = advance(qi, j)
            produce(1 - slot, nqi, nj)
            m_prev = m_ref[qi]
            m_new = jnp.maximum(m_prev, c_refs[slot][...])
            alpha = jnp.exp2(m_prev - m_new)
            p = jnp.exp2(s_refs[slot][...] - m_new)
            l_ref[qi] = alpha * l_ref[qi] + jnp.sum(p, axis=0, keepdims=True)
            m_ref[qi] = m_new
            acc_ref[qi] = alpha * acc_ref[qi] + _dot(vt_ref[0, 0, j], p.astype(BF16))
            return nqi, nj

        def body(_, carry):
            for u in range(ATTN_UNROLL):
                carry = step(u % 2, *carry)
            return carry

        if n_steps > 0:
            first = (jnp.int32(first[0]), jnp.int32(first[1]))
            produce(0, *first)
            carry = lax.fori_loop(0, n_steps // ATTN_UNROLL, body, first)
            for u in range(n_steps % ATTN_UNROLL):
                carry = step(u % 2, *carry)

    run(n_real, (0, 0), lambda qi, j: (qi + 1, j + 1), True)

    def below(qi, j):
        wrap = j + 1 == qi
        return jnp.where(wrap, qi + 1, qi), jnp.where(wrap, 0, j + 1)

    run(n_real * (n_real - 1) // 2, (1, 0), below, False)

    def finish_tile(qi, c):
        qs = pl.multiple_of(qi * ROW_TILE, ROW_TILE)
        o_ref[0, pl.ds(qs, ROW_TILE), :] = (acc_ref[qi] / l_ref[qi]).T
        return c

    lax.fori_loop(0, n_real, finish_tile, 0)


def _attn_call(q, k, vt):
    bsz, _, lp, _ = q.shape
    n_real = lp // ROW_TILE - 1
    qk_spec = pl.BlockSpec((1, 1, lp, QK_PAD), lambda b, h: (b, h, 0, 0))
    return pl.pallas_call(
        functools.partial(_attn_kernel, n_real),
        grid=(bsz, MLA_HEADS),
        in_specs=[qk_spec, qk_spec,
                  pl.BlockSpec((1, 1, n_real + 1, V_DIM, ROW_TILE), lambda b, h: (b, h, 0, 0, 0))],
        out_specs=pl.BlockSpec((1, lp, V_DIM), lambda b, h: (b, 0, h)),
        out_shape=jax.ShapeDtypeStruct((bsz, lp, D_ATTN), F32),
        scratch_shapes=[
            pltpu.VMEM((ROW_TILE, ROW_TILE), F32), pltpu.VMEM((ROW_TILE, ROW_TILE), F32),
            pltpu.VMEM((1, ROW_TILE), F32), pltpu.VMEM((1, ROW_TILE), F32),
            pltpu.VMEM((n_real, 1, ROW_TILE), F32), pltpu.VMEM((n_real, 1, ROW_TILE), F32),
            pltpu.VMEM((n_real, V_DIM, ROW_TILE), F32),
        ],
        compiler_params=_params(),
        name="attn",
    )(q, k, vt)


N_PAIRS = SSM_HEADS // 2


def _split3(a):
    a1 = a.astype(BF16)
    r1 = a - a1.astype(F32)
    a2 = r1.astype(BF16)
    a3 = (r1 - a2.astype(F32)).astype(BF16)
    return a1, a2, a3


def _ssd_kernel(n_chunks, z_ref, xbc_ref, dt_ref, cw_ref, cb_ref, dtb_ref, alog_ref, dskip_ref, ng_ref,
                y_ref, xbuf_ref, state_ref):
    c = pl.program_id(1)

    @pl.when(c == 0)
    def _():
        xbuf_ref[0:SUBLANES, :] = jnp.zeros((SUBLANES, D_XBC), F32)
        state_ref[...] = jnp.zeros(state_ref.shape, F32)

    @pl.when(c >= n_chunks)
    def _():
        y_ref[0] = jnp.zeros((CHUNK, D_SSM), F32)

    @pl.when(c < n_chunks)
    def _():
        x = xbc_ref[0]
        xbuf_ref[SUBLANES:SUBLANES + CHUNK, :] = x
        conv = cb_ref[...] + cw_ref[SSM_CONV - 1:SSM_CONV, :] * x
        for d in range(1, SSM_CONV):
            conv = conv + cw_ref[SSM_CONV - 1 - d:SSM_CONV - d, :] * xbuf_ref[SUBLANES - d:SUBLANES - d + CHUNK, :]
        xbuf_ref[0:SUBLANES, :] = jnp.where(c == 0, x[N_META - SUBLANES:N_META, :], x[CHUNK - SUBLANES:CHUNK, :])
        xc = _silu(conv)
        xs = xc[:, 0:D_SSM]

        row = lax.broadcasted_iota(jnp.int32, (CHUNK, LANES), 0)
        col = lax.broadcasted_iota(jnp.int32, (CHUNK, LANES), 1)
        dt = jax.nn.softplus(dt_ref[0] + dtb_ref[...])
        dt = jnp.where((c > 0) | (row < N_META), dt, 0.0)
        a = dt * (-jnp.exp(alog_ref[...]))
        tri = (col <= row).astype(BF16)
        a1, a2, a3 = _split3(a)
        acs = _dot(tri, a1) + _dot(tri, a2) + _dot(tri, a3)
        acs_t = acs.T
        dt_t = dt.T
        e_acs = jnp.exp(acs)
        last_t = acs_t[:, CHUNK - 1:CHUNK]
        w_t = dt_t * jnp.exp(last_t - acs_t)
        dec_t = jnp.exp(last_t)
        causal = col <= row
        left = col < SSM_P

        for g in range(SSM_GROUPS):
            bm = xc[:, D_SSM + g * SSM_N:D_SSM + (g + 1) * SSM_N]
            cm = xc[:, D_SSM + SSM_GROUPS * SSM_N + g * SSM_N:D_SSM + SSM_GROUPS * SSM_N + (g + 1) * SSM_N]
            bm16 = bm.astype(BF16)
            cm16 = cm.astype(BF16)
            cb = _dot_nt(cm16, bm16)
            bm_t = bm.T
            for pp in range(N_PAIRS // SSM_GROUPS):
                pair = g * (N_PAIRS // SSM_GROUPS) + pp
                xs_pair = xs[:, pair * LANES:(pair + 1) * LANES]
                prev = state_ref[pair]
                rhs_x = [jnp.where(left, xs_pair, 0.0).astype(BF16), jnp.where(left, 0.0, xs_pair).astype(BF16)]
                rhs_s = [jnp.where(left, prev, 0.0).astype(BF16), jnp.where(left, 0.0, prev).astype(BF16)]
                lhs_m, lhs_c, lhs_b, decs = [], [], [], []
                for k in range(2):
                    hd = 2 * pair + k
                    seg = acs[:, hd:hd + 1] - acs_t[hd:hd + 1, :]
                    lm = jnp.exp(jnp.where(causal, seg, NEG))
                    lhs_m.append((cb * lm * dt_t[hd:hd + 1, :]).astype(BF16))
                    lhs_c.append((cm * e_acs[:, hd:hd + 1]).astype(BF16))
                    lhs_b.append((bm_t * w_t[hd:hd + 1, :]).astype(BF16))
                    decs.append(jnp.broadcast_to(dec_t[hd:hd + 1, :], (SSM_N, LANES)))
                y_pair = _dot(jnp.concatenate(lhs_m + lhs_c, axis=1), jnp.concatenate(rhs_x + rhs_s, axis=0))
                new = _dot(jnp.concatenate(lhs_b, axis=1), jnp.concatenate(rhs_x, axis=0))
                state_ref[pair] = jnp.where(left, decs[0], decs[1]) * prev + new
                y_ref[0, :, pair * LANES:(pair + 1) * LANES] = y_pair

        y = (y_ref[0] + dskip_ref[...] * xs) * _silu(z_ref[0])
        gsz = D_SSM // SSM_GROUPS
        for g in range(SSM_GROUPS):
            yg = y[:, g * gsz:(g + 1) * gsz]
            ms = jnp.mean(yg * yg, axis=-1, keepdims=True)
            y_ref[0, :, g * gsz:(g + 1) * gsz] = yg * lax.rsqrt(ms + EPS) * ng_ref[:, g * gsz:(g + 1) * gsz]


def _ssd_call(z, xbc, dt, conv_w, conv_b, dt_bias, a_log, d_skip, norm_gain):
    bsz, lp, _ = z.shape
    n_blocks = lp // CHUNK
    pre_blk = (lp - ROW_TILE) // CHUNK
    n_chunks = pre_blk + 1

    def blk(b, c):
        return (b, jnp.where(c == 0, pre_blk, jnp.where(c < n_chunks, c - 1, c)), 0)

    row = lambda w: pl.BlockSpec((1, CHUNK, w), blk)
    return pl.pallas_call(
        functools.partial(_ssd_kernel, n_chunks),
        grid=(bsz, n_blocks),
        in_specs=[row(D_SSM), row(D_XBC), row(LANES),
                  _const_spec((SSM_CONV, D_XBC)), _const_spec((1, D_XBC)), _const_spec((1, LANES)),
                  _const_spec((1, LANES)), _const_spec((1, D_SSM)), _const_spec((1, D_SSM))],
        out_specs=row(D_SSM),
        out_shape=jax.ShapeDtypeStruct((bsz, lp, D_SSM), F32),
        scratch_shapes=[pltpu.VMEM((SUBLANES + CHUNK, D_XBC), F32), pltpu.VMEM((N_PAIRS, SSM_N, LANES), F32)],
        compiler_params=_params(),
        name="ssd",
    )(z, xbc, dt, conv_w, conv_b, dt_bias, a_log, d_skip, norm_gain)


def _mix_kernel(n_real, x_ref, meta_ref, attn_ref, ssm_ref, ga_ref, wout_ref, gp_ref, h1_ref):
    i = pl.program_id(1)
    h = jnp.where(i == n_real, meta_ref[...], x_ref[0])
    an = _rms(attn_ref[0], ga_ref[...]).astype(BF16)
    mix = _dot(an, wout_ref[0:D_ATTN, :]) + _dot(ssm_ref[0].astype(BF16), wout_ref[D_ATTN:, :])
    h1_ref[0] = h + _rms(mix, gp_ref[...])


def _mix_call(x, meta_tile, attn, ssm, gain_attn, w_out, gain_post):
    bsz, lp, _ = attn.shape
    n_real = lp // ROW_TILE - 1
    row = pl.BlockSpec((1, ROW_TILE, D_MODEL), lambda b, i: (b, i, 0))
    return pl.pallas_call(
        functools.partial(_mix_kernel, n_real),
        grid=(bsz, n_real + 1),
        in_specs=[pl.BlockSpec((1, ROW_TILE, D_MODEL), lambda b, i: (b, jnp.minimum(i, n_real - 1), 0)),
                  _const_spec((ROW_TILE, D_MODEL)), row, row,
                  _const_spec((1, D_ATTN)), _const_spec((D_ATTN + D_SSM, D_MODEL)), _const_spec((1, D_MODEL))],
        out_specs=row,
        out_shape=jax.ShapeDtypeStruct((bsz, lp, D_MODEL), F32),
        compiler_params=_params(),
        name="mix",
    )(x, meta_tile, attn, ssm, gain_attn, w_out, gain_post)


def _shifted(u, tail, d):
    r = pltpu.roll(u, d, 0)
    row = lax.broadcasted_iota(jnp.int32, (SUBLANES, u.shape[1]), 0)
    head = jnp.where(row < d, pltpu.roll(tail, d, 0), r[0:SUBLANES])
    return jnp.concatenate([head, r[SUBLANES:]], axis=0)


def _ffn_kernel(h1_ref, h1m_ref, gpre_ref, wg_ref, wv_ref, cwg_ref, cwv_ref, wd_ref, gpost_ref,
                out_ref, tg_ref, tv_ref, acc_ref):
    i = pl.program_id(1)
    g_pre = gpre_ref[...]

    @pl.when(i == 0)
    def _():
        hm = _rms(h1m_ref[0, N_META - SUBLANES:N_META, :], g_pre).astype(BF16)

        def seed(c, carry):
            tg_ref[c] = _dot(hm, wg_ref[c])
            tv_ref[c] = _dot(hm, wv_ref[c])
            return carry

        lax.fori_loop(0, N_FF_CHUNKS, seed, 0)

    h1 = h1_ref[0]
    hn = _rms(h1, g_pre).astype(BF16)
    acc_ref[...] = jnp.zeros(acc_ref.shape, F32)

    def conv(u, tail, cw):
        return (cw[FFN_CONV:FFN_CONV + 1, :] + cw[2:3, :] * u + cw[1:2, :] * _shifted(u, tail, 1)
                + cw[0:1, :] * _shifted(u, tail, 2))

    def chunk(c, carry):
        ug = _dot(hn, wg_ref[c])
        uv = _dot(hn, wv_ref[c])
        cg = conv(ug, tg_ref[c], cwg_ref[c])
        cv = conv(uv, tv_ref[c], cwv_ref[c])
        tg_ref[c] = ug[ROW_TILE - SUBLANES:ROW_TILE, :]
        tv_ref[c] = uv[ROW_TILE - SUBLANES:ROW_TILE, :]
        act = (_silu(cg) * cv).astype(BF16)
        acc_ref[...] += _dot(act, wd_ref[c])
        return carry

    lax.fori_loop(0, N_FF_CHUNKS, chunk, 0)
    out_ref[0] = h1 + _rms(acc_ref[...], gpost_ref[...])


def _ffn_call(h1, gain_pre, w_gate, w_val, cw_gate, cw_val, w_down, gain_post):
    bsz, lp, _ = h1.shape
    seq = lp - ROW_TILE
    return pl.pallas_call(
        _ffn_kernel,
        grid=(bsz, seq // ROW_TILE),
        in_specs=[pl.BlockSpec((1, ROW_TILE, D_MODEL), lambda b, i: (b, i, 0)),
                  pl.BlockSpec((1, BLK, D_MODEL), lambda b, i: (b, seq // BLK, 0)),
                  _const_spec((1, D_MODEL)),
                  _const_spec((N_FF_CHUNKS, D_MODEL, FF_CHUNK)), _const_spec((N_FF_CHUNKS, D_MODEL, FF_CHUNK)),
                  _const_spec((N_FF_CHUNKS, SUBLANES, FF_CHUNK)), _const_spec((N_FF_CHUNKS, SUBLANES, FF_CHUNK)),
                  _const_spec((N_FF_CHUNKS, FF_CHUNK, D_MODEL)), _const_spec((1, D_MODEL))],
        out_specs=pl.BlockSpec((1, ROW_TILE, D_MODEL), lambda b, i: (b, i, 0)),
        out_shape=jax.ShapeDtypeStruct((bsz, seq, D_MODEL), F32),
        scratch_shapes=[pltpu.VMEM((N_FF_CHUNKS, SUBLANES, FF_CHUNK), F32),
                        pltpu.VMEM((N_FF_CHUNKS, SUBLANES, FF_CHUNK), F32),
                        pltpu.VMEM((ROW_TILE, D_MODEL), F32)],
        compiler_params=_params(),
        name="ffn",
    )(h1, h1, gain_pre, w_gate, w_val, cw_gate, cw_val, w_down, gain_post)


def _swap_halves(t):
    half = t.shape[-1] // 2
    return jnp.concatenate([t[..., half:], t[..., :half]], axis=-1)


def _rope_tables(seq, lp):
    n = N_META + seq
    inv = ROPE_THETA ** (-jnp.arange(0, QK_ROPE, 2, dtype=F32) / QK_ROPE)
    ang = jnp.arange(n, dtype=F32)[:, None] * inv[None, :]
    cos, sin = jnp.cos(ang), jnp.sin(ang)

    def place(t):
        out = jnp.zeros((lp, t.shape[1]), F32)
        out = out.at[0:seq].set(t[N_META:])
        return out.at[seq:seq + N_META].set(t[:N_META])

    zeros = jnp.zeros((n, QK_ROPE), F32)
    q_cos = place(jnp.concatenate([cos, cos, zeros], axis=1))
    q_sin = place(jnp.concatenate([-sin, sin, zeros], axis=1))
    k_tab = place(jnp.concatenate([cos, cos, -sin, sin], axis=1))
    return q_cos, q_sin, k_tab


def _pad_lanes(v, width=LANES):
    return jnp.zeros((1, width), F32).at[0, :v.shape[0]].set(v)


def kernel(x, meta_tokens, norm_mix_pre, norm_mix_post, norm_ffn_pre, norm_ffn_post, w_in, q_a_norm, w_uq,
           kv_a_norm, w_ukv, attn_out_norm, ssm_conv_w, ssm_conv_b, ssm_dt_bias, ssm_A_log, ssm_D, ssm_norm,
           w_out, w_up, ffn_conv_w, ffn_conv_b, w_down):
    bsz, seq, _ = x.shape
    lp = seq + ROW_TILE
    l = 0
    meta_tile = jnp.zeros((ROW_TILE, D_MODEL), F32).at[0:N_META].set(meta_tokens.astype(F32))

    o_kv, o_pe, o_z = Q_RANK, Q_RANK + KV_RANK, Q_RANK + KV_RANK + QK_ROPE
    o_xbc, o_dt = o_z + D_SSM, o_z + D_SSM + D_XBC
    wi = w_in[l]
    w_pe = wi[:, o_pe:o_z]
    w_in_p = jnp.concatenate(
        [wi[:, :o_pe], w_pe, _swap_halves(w_pe), wi[:, o_z:o_dt],
         wi[:, o_dt:], jnp.zeros((D_MODEL, LANES - SSM_HEADS), F32)], axis=1).astype(BF16)

    wq = w_uq[l] * (SOFTMAX_SCALE * math.log2(math.e))
    wq_nope = wq[:, :, :QK_NOPE].reshape(Q_RANK, D_ATTN).astype(BF16)
    pad = jnp.zeros((Q_RANK, MLA_HEADS, LANES - QK_ROPE), F32)
    wq_pe = jnp.concatenate([wq[:, :, QK_NOPE:], pad], axis=2).reshape(Q_RANK, D_ATTN).astype(BF16)
    wq_sw = jnp.concatenate([_swap_halves(wq[:, :, QK_NOPE:]), pad], axis=2).reshape(Q_RANK, D_ATTN).astype(BF16)
    wkv = w_ukv[l]
    w_kv = jnp.concatenate([wkv[:, :, :QK_NOPE].reshape(KV_RANK, D_ATTN),
                            wkv[:, :, QK_NOPE:].reshape(KV_RANK, D_ATTN)], axis=1).astype(BF16)
    q_cos, q_sin, k_tab = _rope_tables(seq, lp)

    q, k, vt, z, xbc, dt = _front_call(
        x, meta_tile, norm_mix_pre[l][None], w_in_p, q_a_norm[l][None], wq_nope, wq_pe, wq_sw,
        kv_a_norm[l][None], w_kv, q_cos, q_sin, k_tab)

    attn = _attn_call(q, k, vt)

    ssm = _ssd_call(z, xbc, dt, ssm_conv_w[l], ssm_conv_b[l][None], _pad_lanes(ssm_dt_bias[l]),
                    _pad_lanes(ssm_A_log[l]), jnp.repeat(ssm_D[l], SSM_P)[None], ssm_norm[l][None])

    h1 = _mix_call(x, meta_tile, attn, ssm, attn_out_norm[l][None], w_out[l].astype(BF16),
                   norm_mix_post[l][None])

    def chunks_cols(w):
        return jnp.transpose(w.reshape(w.shape[0], N_FF_CHUNKS, FF_CHUNK), (1, 0, 2))

    wu = w_up[l]
    cw = jnp.concatenate([ffn_conv_w[l], ffn_conv_b[l][None],
                          jnp.zeros((SUBLANES - FFN_CONV - 1, 2 * D_FF), F32)], axis=0)
    return _ffn_call(
        h1, norm_ffn_pre[l][None],
        chunks_cols(wu[:, :D_FF]).astype(BF16), chunks_cols(wu[:, D_FF:]).astype(BF16),
        chunks_cols(cw[:, :D_FF]), chunks_cols(cw[:, D_FF:]),
        w_down[l].reshape(N_FF_CHUNKS, FF_CHUNK, D_MODEL).astype(BF16), norm_ffn_post[l][None])
```

```python
import functools
import math

import jax
import jax.numpy as jnp
import numpy as np
from jax import lax
from jax.experimental import pallas as pl
from jax.experimental.pallas import tpu as pltpu

F32 = jnp.float32
BF16 = jnp.bfloat16

D_MODEL = 1024
N_META = 16
MLA_HEADS = 8
QK_NOPE = 128
QK_ROPE = 64
V_DIM = 128
Q_RANK = 384
KV_RANK = 256
ROPE_THETA = 10000.0
SOFTMAX_SCALE = (QK_NOPE + QK_ROPE) ** -0.5
D_ATTN = MLA_HEADS * V_DIM
SSM_HEADS = 16
SSM_P = 64
SSM_GROUPS = 2
SSM_N = 128
SSM_CONV = 4
CHUNK = 128
D_SSM = SSM_HEADS * SSM_P
D_XBC = D_SSM + 2 * SSM_GROUPS * SSM_N
D_FF = 2816
FFN_CONV = 3
EPS = 1e-6

LANES = 128
SUBLANES = 8
ROW_TILE = 512
BLK = 128
QK_PAD = 256
FF_CHUNK = 256
FFN_TILE = 512
ATTN_UNROLL = 8
N_FF_CHUNKS = D_FF // FF_CHUNK
LAT_W = 768
IN_W = LAT_W + D_SSM + D_XBC + LANES
NEG = -1e30
VMEM_LIMIT = 56 * 1024 * 1024


def _dot(a, b):
    return jnp.dot(a, b, preferred_element_type=F32)


def _dot_nt(a, b):
    return lax.dot_general(a, b, (((1,), (1,)), ((), ())), preferred_element_type=F32)


def _rms(x, gain):
    ms = jnp.mean(x * x, axis=-1, keepdims=True)
    return x * lax.rsqrt(ms + EPS) * gain


def _silu(x):
    return x * (1.0 / (1.0 + jnp.exp(-x)))


def _const_spec(shape):
    nd = len(shape)
    return pl.BlockSpec(shape, lambda *_: (0,) * nd, pipeline_mode=pl.Buffered(1))


def _params():
    return pltpu.CompilerParams(dimension_semantics=("arbitrary", "arbitrary"),
                                vmem_limit_bytes=VMEM_LIMIT)


def _front_kernel(n_real, x_ref, meta_ref, g_ref, win_ref, qan_ref, wq_ref, kvan_ref, wkv_ref, tab_ref,
                  q_ref, k_ref, vt_ref, z_ref, xbc_ref, dt_ref):
    i = pl.program_id(1)
    h = jnp.where(i == n_real, meta_ref[...], x_ref[0])
    hn = _rms(h, g_ref[...]).astype(BF16)
    lat = _dot(hn, win_ref[:, 0:LAT_W])
    o = LAT_W
    z_ref[0] = _dot(hn, win_ref[:, o:o + D_SSM])
    o += D_SSM
    xbc_ref[0] = _dot(hn, win_ref[:, o:o + D_XBC])
    o += D_XBC
    dt_ref[0] = _dot(hn, win_ref[:, o:o + LANES])

    tab = tab_ref[...]
    lane = lax.broadcasted_iota(jnp.int32, tab.shape, 1)

    def rope(pair):
        a = pair * tab
        return jnp.where(lane < QK_ROPE, a + pltpu.roll(a, QK_ROPE, 1), 0.0).astype(BF16)

    qn = _rms(lat[:, 0:Q_RANK], qan_ref[...]).astype(BF16)
    q = _dot(qn, wq_ref[...])
    kvn = _rms(lat[:, Q_RANK:Q_RANK + KV_RANK], kvan_ref[...]).astype(BF16)
    kv = _dot(kvn, wkv_ref[...])
    k_pe = rope(lat[:, Q_RANK + KV_RANK:LAT_W])
    for hd in range(MLA_HEADS):
        sl = slice(hd * LANES, (hd + 1) * LANES)
        q_ref[0, hd, :, 0:LANES] = q[:, sl].astype(BF16)
        q_ref[0, hd, :, LANES:QK_PAD] = rope(q[:, D_ATTN + hd * LANES:D_ATTN + (hd + 1) * LANES])
        k_ref[0, hd, :, 0:LANES] = kv[:, sl].astype(BF16)
        k_ref[0, hd, :, LANES:QK_PAD] = k_pe
        v = kv[:, D_ATTN + hd * LANES:D_ATTN + (hd + 1) * LANES]
        vt_ref[0, hd, 0] = v.T.astype(BF16)


def _front_call(x, meta_tile, gain, w_in, q_an, w_q, kv_an, w_kv, rope_tab):
    bsz, seq, _ = x.shape
    n_real = seq // ROW_TILE
    n_tiles = n_real + 1
    lp = n_tiles * ROW_TILE
    row = lambda w: pl.BlockSpec((1, ROW_TILE, w), lambda b, i: (b, i, 0))
    tab = pl.BlockSpec((ROW_TILE, LANES), lambda b, i: (i, 0))
    qk_spec = pl.BlockSpec((1, MLA_HEADS, ROW_TILE, QK_PAD), lambda b, i: (b, 0, i, 0))
    return pl.pallas_call(
        functools.partial(_front_kernel, n_real),
        grid=(bsz, n_tiles),
        in_specs=[
            pl.BlockSpec((1, ROW_TILE, D_MODEL), lambda b, i: (b, jnp.minimum(i, n_real - 1), 0)),
            _const_spec((ROW_TILE, D_MODEL)),
            _const_spec((1, D_MODEL)),
            _const_spec((D_MODEL, IN_W)),
            _const_spec((1, Q_RANK)),
            _const_spec((Q_RANK, 2 * D_ATTN)),
            _const_spec((1, KV_RANK)),
            _const_spec((KV_RANK, 2 * D_ATTN)),
            tab,
        ],
        out_specs=[
            qk_spec, qk_spec,
            pl.BlockSpec((1, MLA_HEADS, 1, V_DIM, ROW_TILE), lambda b, i: (b, 0, i, 0, 0)),
            row(D_SSM), row(D_XBC), row(LANES),
        ],
        out_shape=[
            jax.ShapeDtypeStruct((bsz, MLA_HEADS, lp, QK_PAD), BF16),
            jax.ShapeDtypeStruct((bsz, MLA_HEADS, lp, QK_PAD), BF16),
            jax.ShapeDtypeStruct((bsz, MLA_HEADS, n_tiles, V_DIM, ROW_TILE), BF16),
            jax.ShapeDtypeStruct((bsz, lp, D_SSM), F32),
            jax.ShapeDtypeStruct((bsz, lp, D_XBC), F32),
            jax.ShapeDtypeStruct((bsz, lp, LANES), F32),
        ],
        compiler_params=_params(),
        name="front",
    )(x, meta_tile, gain, w_in, q_an, w_q, kv_an, w_kv, rope_tab)


def _attn_kernel(n_real, q_ref, k_ref, vt_ref, o_ref, s0_ref, s1_ref, c0_ref, c1_ref, m_ref, l_ref, acc_ref):
    pre = n_real * ROW_TILE
    s_refs, c_refs = (s0_ref, s1_ref), (c0_ref, c1_ref)
    k_meta = k_ref[0, 0, pre:pre + N_META, :]
    vt_meta = vt_ref[0, 0, n_real, :, 0:N_META]

    def softmax_cols(s):
        m = jnp.max(s, axis=0, keepdims=True)
        p = jnp.exp2(s - m)
        return m, jnp.sum(p, axis=0, keepdims=True), p

    s = _dot_nt(k_meta, q_ref[0, 0, pre:pre + BLK, :])
    kk = lax.broadcasted_iota(jnp.int32, s.shape, 0)
    qq = lax.broadcasted_iota(jnp.int32, s.shape, 1)
    _, l, p = softmax_cols(jnp.where(kk <= qq, s, NEG))
    o_ref[0, pre:pre + BLK, :] = (_dot(vt_meta, p.astype(BF16)) / l).T
    o_ref[0, pre + BLK:, :] = jnp.zeros((ROW_TILE - BLK, V_DIM), F32)

    m, l, p = softmax_cols(_dot_nt(k_meta, q_ref[0, 0, 0:pre, :]))
    acc = _dot(vt_meta, p.astype(BF16))
    for qi in range(n_real):
        cols = slice(qi * ROW_TILE, (qi + 1) * ROW_TILE)
        m_ref[qi] = m[:, cols]
        l_ref[qi] = l[:, cols]
        acc_ref[qi] = acc[:, cols]

    def scores(qi, j):
        qs = pl.multiple_of(qi * ROW_TILE, ROW_TILE)
        ks = pl.multiple_of(j * ROW_TILE, ROW_TILE)
        return _dot_nt(k_ref[0, 0, pl.ds(ks, ROW_TILE), :], q_ref[0, 0, pl.ds(qs, ROW_TILE), :])

    def run(n_steps, first, advance, diagonal):
        def produce(slot, qi, j):
            s = scores(jnp.minimum(qi, n_real - 1), jnp.minimum(j, n_real - 1))
            if diagonal:
                kd = lax.broadcasted_iota(jnp.int32, s.shape, 0)
                qd = lax.broadcasted_iota(jnp.int32, s.shape, 1)
                s = jnp.where(kd <= qd, s, NEG)
            s_refs[slot][...] = s
            c_refs[slot][...] = jnp.max(s, axis=0, keepdims=True)

        def step(slot, qi, j):
            nqi, nj = advance(qi, j)
            produce(1 - slot, nqi, nj)
            m_prev = m_ref[qi]
            m_new = jnp.maximum(m_prev, c_refs[slot][...])
            alpha = jnp.exp2(m_prev - m_new)
            p = jnp.exp2(s_refs[slot][...] - m_new)
            l_ref[qi] = alpha * l_ref[qi] + jnp.sum(p, axis=0, keepdims=True)
            m_ref[qi] = m_new
            acc_ref[qi] = alpha * acc_ref[qi] + _dot(vt_ref[0, 0, j], p.astype(BF16))
            return nqi, nj

        def body(_, carry):
            for u in range(ATTN_UNROLL):
                carry = step(u % 2, *carry)
            return carry

        if n_steps > 0:
            first = (jnp.int32(first[0]), jnp.int32(first[1]))
            produce(0, *first)
            carry = lax.fori_loop(0, n_steps // ATTN_UNROLL, body, first)
            for u in range(n_steps % ATTN_UNROLL):
                carry = step(u % 2, *carry)

    run(n_real, (0, 0), lambda qi, j: (qi + 1, j + 1), True)

    def below(qi, j):
        wrap = j + 1 == qi
        return jnp.where(wrap, qi + 1, qi), jnp.where(wrap, 0, j + 1)

    run(n_real * (n_real - 1) // 2, (1, 0), below, False)

    def finish_tile(qi, c):
        qs = pl.multiple_of(qi * ROW_TILE, ROW_TILE)
        o_ref[0, pl.ds(qs, ROW_TILE), :] = (acc_ref[qi] / l_ref[qi]).T
        return c

    lax.fori_loop(0, n_real, finish_tile, 0)


def _attn_call(q, k, vt):
    bsz, _, lp, _ = q.shape
    n_real = lp // ROW_TILE - 1
    qk_spec = pl.BlockSpec((1, 1, lp, QK_PAD), lambda b, h: (b, h, 0, 0))
    return pl.pallas_call(
        functools.partial(_attn_kernel, n_real),
        grid=(bsz, MLA_HEADS),
        in_specs=[qk_spec, qk_spec,
                  pl.BlockSpec((1, 1, n_real + 1, V_DIM, ROW_TILE), lambda b, h: (b, h, 0, 0, 0))],
        out_specs=pl.BlockSpec((1, lp, V_DIM), lambda b, h: (b, 0, h)),
        out_shape=jax.ShapeDtypeStruct((bsz, lp, D_ATTN), F32),
        scratch_shapes=[
            pltpu.VMEM((ROW_TILE, ROW_TILE), F32), pltpu.VMEM((ROW_TILE, ROW_TILE), F32),
            pltpu.VMEM((1, ROW_TILE), F32), pltpu.VMEM((1, ROW_TILE), F32),
            pltpu.VMEM((n_real, 1, ROW_TILE), F32), pltpu.VMEM((n_real, 1, ROW_TILE), F32),
            pltpu.VMEM((n_real, V_DIM, ROW_TILE), F32),
        ],
        compiler_params=_params(),
        name="attn",
    )(q, k, vt)


N_PAIRS = SSM_HEADS // 2


def _split3(a):
    a1 = a.astype(BF16)
    r1 = a - a1.astype(F32)
    a2 = r1.astype(BF16)
    a3 = (r1 - a2.astype(F32)).astype(BF16)
    return a1, a2, a3


def _ssd_kernel(n_chunks, z_ref, xbc_ref, dt_ref, cw_ref, cb_ref, dtb_ref, alog_ref, dskip_ref, ng_ref,
                y_ref, xbuf_ref, state_ref):
    c = pl.program_id(1)

    @pl.when(c == 0)
    def _():
        xbuf_ref[0:SUBLANES, :] = jnp.zeros((SUBLANES, D_XBC), F32)
        state_ref[...] = jnp.zeros(state_ref.shape, F32)

    @pl.when(c >= n_chunks)
    def _():
        y_ref[0] = jnp.zeros((CHUNK, D_SSM), F32)

    @pl.when(c < n_chunks)
    def _():
        x = xbc_ref[0]
        xbuf_ref[SUBLANES:SUBLANES + CHUNK, :] = x
        conv = cb_ref[...] + cw_ref[SSM_CONV - 1:SSM_CONV, :] * x
        for d in range(1, SSM_CONV):
            conv = conv + cw_ref[SSM_CONV - 1 - d:SSM_CONV - d, :] * xbuf_ref[SUBLANES - d:SUBLANES - d + CHUNK, :]
        xbuf_ref[0:SUBLANES, :] = jnp.where(c == 0, x[N_META - SUBLANES:N_META, :], x[CHUNK - SUBLANES:CHUNK, :])
        xc = _silu(conv)
        xs = xc[:, 0:D_SSM]

        row = lax.broadcasted_iota(jnp.int32, (CHUNK, LANES), 0)
        col = lax.broadcasted_iota(jnp.int32, (CHUNK, LANES), 1)
        dt = jax.nn.softplus(dt_ref[0] + dtb_ref[...])
        dt = jnp.where((c > 0) | (row < N_META), dt, 0.0)
        a = dt * (-jnp.exp(alog_ref[...]))
        tri = (col <= row).astype(BF16)
        a1, a2, a3 = _split3(a)
        acs = _dot(tri, a1) + _dot(tri, a2) + _dot(tri, a3)
        acs_t = acs.T
        dt_t = dt.T
        e_acs = jnp.exp(acs)
        last_t = acs_t[:, CHUNK - 1:CHUNK]
        w_t = dt_t * jnp.exp(last_t - acs_t)
        dec_t = jnp.exp(last_t)
        causal = col <= row
        left = col < SSM_P

        for g in range(SSM_GROUPS):
            bm = xc[:, D_SSM + g * SSM_N:D_SSM + (g + 1) * SSM_N]
            cm = xc[:, D_SSM + SSM_GROUPS * SSM_N + g * SSM_N:D_SSM + SSM_GROUPS * SSM_N + (g + 1) * SSM_N]
            bm16 = bm.astype(BF16)
            cm16 = cm.astype(BF16)
            cb = _dot_nt(cm16, bm16)
            bm_t = bm.T
            for pp in range(N_PAIRS // SSM_GROUPS):
                pair = g * (N_PAIRS // SSM_GROUPS) + pp
                xs_pair = xs[:, pair * LANES:(pair + 1) * LANES]
                prev = state_ref[pair]
                rhs_x = [jnp.where(left, xs_pair, 0.0).astype(BF16), jnp.where(left, 0.0, xs_pair).astype(BF16)]
                rhs_s = [jnp.where(left, prev, 0.0).astype(BF16), jnp.where(left, 0.0, prev).astype(BF16)]
                lhs_m, lhs_c, lhs_b, decs = [], [], [], []
                for k in range(2):
                    hd = 2 * pair + k
                    seg = acs[:, hd:hd + 1] - acs_t[hd:hd + 1, :]
                    lm = jnp.exp(jnp.where(causal, seg, NEG))
                    lhs_m.append((cb * lm * dt_t[hd:hd + 1, :]).astype(BF16))
                    lhs_c.append((cm * e_acs[:, hd:hd + 1]).astype(BF16))
                    lhs_b.append((bm_t * w_t[hd:hd + 1, :]).astype(BF16))
                    decs.append(jnp.broadcast_to(dec_t[hd:hd + 1, :], (SSM_N, LANES)))
                y_pair = _dot(jnp.concatenate(lhs_m + lhs_c, axis=1), jnp.concatenate(rhs_x + rhs_s, axis=0))
                new = _dot(jnp.concatenate(lhs_b, axis=1), jnp.concatenate(rhs_x, axis=0))
                state_ref[pair] = jnp.where(left, decs[0], decs[1]) * prev + new
                y_ref[0, :, pair * LANES:(pair + 1) * LANES] = y_pair

        y = (y_ref[0] + dskip_ref[...] * xs) * _silu(z_ref[0])
        gsz = D_SSM // SSM_GROUPS
        for g in range(SSM_GROUPS):
            yg = y[:, g * gsz:(g + 1) * gsz]
            ms = jnp.mean(yg * yg, axis=-1, keepdims=True)
            y_ref[0, :, g * gsz:(g + 1) * gsz] = yg * lax.rsqrt(ms + EPS) * ng_ref[:, g * gsz:(g + 1) * gsz]


def _ssd_call(z, xbc, dt, conv_w, conv_b, dt_bias, a_log, d_skip, norm_gain):
    bsz, lp, _ = z.shape
    n_blocks = lp // CHUNK
    pre_blk = (lp - ROW_TILE) // CHUNK
    n_chunks = pre_blk + 1

    def blk(b, c):
        return (b, jnp.where(c == 0, pre_blk, jnp.where(c < n_chunks, c - 1, c)), 0)

    row = lambda w: pl.BlockSpec((1, CHUNK, w), blk)
    return pl.pallas_call(
        functools.partial(_ssd_kernel, n_chunks),
        grid=(bsz, n_blocks),
        in_specs=[row(D_SSM), row(D_XBC), row(LANES),
                  _const_spec((SSM_CONV, D_XBC)), _const_spec((1, D_XBC)), _const_spec((1, LANES)),
                  _const_spec((1, LANES)), _const_spec((1, D_SSM)), _const_spec((1, D_SSM))],
        out_specs=row(D_SSM),
        out_shape=jax.ShapeDtypeStruct((bsz, lp, D_SSM), F32),
        scratch_shapes=[pltpu.VMEM((SUBLANES + CHUNK, D_XBC), F32), pltpu.VMEM((N_PAIRS, SSM_N, LANES), F32)],
        compiler_params=_params(),
        name="ssd",
    )(z, xbc, dt, conv_w, conv_b, dt_bias, a_log, d_skip, norm_gain)


def _shifted(u, tail, d):
    r = pltpu.roll(u, d, 0)
    row = lax.broadcasted_iota(jnp.int32, (SUBLANES, u.shape[1]), 0)
    head = jnp.where(row < d, pltpu.roll(tail, d, 0), r[0:SUBLANES])
    return jnp.concatenate([head, r[SUBLANES:]], axis=0)


def _tail_kernel(x_ref, attn_ref, ssm_ref, xm_ref, attnm_ref, ssmm_ref, ga_ref, wout_ref, gmix_ref,
                 gpre_ref, wu_ref, cw_ref, wd_ref, gpost_ref, out_ref, tail_ref):
    i = pl.program_id(1)
    g_pre = gpre_ref[...]
    cols = lambda c, half: slice(half * D_FF + c * FF_CHUNK, half * D_FF + (c + 1) * FF_CHUNK)

    def mixed(h, attn, ssm):
        an = _rms(attn, ga_ref[...]).astype(BF16)
        mix = _dot(an, wout_ref[0:D_ATTN, :]) + _dot(ssm.astype(BF16), wout_ref[D_ATTN:, :])
        return h + _rms(mix, gmix_ref[...])

    @pl.when(i == 0)
    def _():
        rows = slice(N_META - SUBLANES, N_META)
        hm = _rms(mixed(xm_ref[...], attnm_ref[0, rows, :], ssmm_ref[0, rows, :]), g_pre).astype(BF16)
        for c in range(2 * N_FF_CHUNKS):
            sl = slice(c * FF_CHUNK, (c + 1) * FF_CHUNK)
            tail_ref[:, sl] = _dot(hm, wu_ref[:, sl])

    h1 = mixed(x_ref[0], attn_ref[0], ssm_ref[0])
    hn = _rms(h1, g_pre).astype(BF16)

    def conv(u, sl):
        tail = tail_ref[:, sl]
        tail_ref[:, sl] = u[FFN_TILE - SUBLANES:FFN_TILE, :]
        return (cw_ref[FFN_CONV:FFN_CONV + 1, sl] + cw_ref[2:3, sl] * u + cw_ref[1:2, sl] * _shifted(u, tail, 1)
                + cw_ref[0:1, sl] * _shifted(u, tail, 2))

    def up(c):
        return _dot(hn, wu_ref[:, cols(c, 0)]), _dot(hn, wu_ref[:, cols(c, 1)])

    u = up(0)
    down = None
    for c in range(N_FF_CHUNKS):
        u_next = up(c + 1) if c + 1 < N_FF_CHUNKS else None
        act = (_silu(conv(u[0], cols(c, 0))) * conv(u[1], cols(c, 1))).astype(BF16)
        d = _dot(act, wd_ref[c * FF_CHUNK:(c + 1) * FF_CHUNK, :])
        down = d if down is None else down + d
        u = u_next
    out_ref[0] = h1 + _rms(down, gpost_ref[...])


def _tail_call(x, attn, ssm, meta_rows, gain_attn, w_out, gain_mix, gain_pre, w_up, conv_wb, w_down, gain_post):
    bsz, seq, _ = x.shape
    row = pl.BlockSpec((1, FFN_TILE, D_MODEL), lambda b, i: (b, i, 0))
    prefix = pl.BlockSpec((1, BLK, D_MODEL), lambda b, i: (b, seq // BLK, 0))
    vec = _const_spec((1, D_MODEL))
    return pl.pallas_call(
        _tail_kernel,
        grid=(bsz, seq // FFN_TILE),
        in_specs=[row, row, row, _const_spec((SUBLANES, D_MODEL)), prefix, prefix,
                  vec, _const_spec((D_ATTN + D_SSM, D_MODEL)), vec, vec,
                  _const_spec((D_MODEL, 2 * D_FF)), _const_spec((SUBLANES, 2 * D_FF)),
                  _const_spec((D_FF, D_MODEL)), vec],
        out_specs=row,
        out_shape=jax.ShapeDtypeStruct((bsz, seq, D_MODEL), F32),
        scratch_shapes=[pltpu.VMEM((SUBLANES, 2 * D_FF), F32)],
        compiler_params=_params(),
        name="tail",
    )(x, attn, ssm, meta_rows, attn, ssm, gain_attn, w_out, gain_mix, gain_pre, w_up, conv_wb, w_down, gain_post)


def _swap_halves(t):
    half = t.shape[-1] // 2
    return jnp.concatenate([t[..., half:], t[..., :half]], axis=-1)


def _rope_table(seq, lp):
    inv = ROPE_THETA ** (-jnp.arange(0, QK_ROPE, 2, dtype=F32) / QK_ROPE)

    def cs(pos):
        ang = pos.astype(F32)[:, None] * inv[None, :]
        return jnp.cos(ang), jnp.sin(ang)

    c_off, s_off = cs(N_META + ROW_TILE * jnp.arange(seq // ROW_TILE))
    c_row, s_row = cs(jnp.arange(ROW_TILE))
    cos = (c_off[:, None] * c_row[None] - s_off[:, None] * s_row[None]).reshape(seq, -1)
    sin = (s_off[:, None] * c_row[None] + c_off[:, None] * s_row[None]).reshape(seq, -1)
    c_meta, s_meta = cs(jnp.arange(N_META))
    tab = lambda c, s: jnp.concatenate([c, c, -s, s], axis=1)
    return jnp.concatenate([tab(cos, sin), tab(c_meta, s_meta),
                            jnp.zeros((lp - seq - N_META, LANES), F32)], axis=0)


def _pad_lanes(v, width=LANES):
    return jnp.zeros((1, width), F32).at[0, :v.shape[0]].set(v)


def kernel(x, meta_tokens, norm_mix_pre, norm_mix_post, norm_ffn_pre, norm_ffn_post, w_in, q_a_norm, w_uq,
           kv_a_norm, w_ukv, attn_out_norm, ssm_conv_w, ssm_conv_b, ssm_dt_bias, ssm_A_log, ssm_D, ssm_norm,
           w_out, w_up, ffn_conv_w, ffn_conv_b, w_down):
    bsz, seq, _ = x.shape
    lp = seq + ROW_TILE
    l = 0
    meta_tile = jnp.zeros((ROW_TILE, D_MODEL), F32).at[0:N_META].set(meta_tokens.astype(F32))

    o_kv, o_pe, o_z = Q_RANK, Q_RANK + KV_RANK, Q_RANK + KV_RANK + QK_ROPE
    o_xbc, o_dt = o_z + D_SSM, o_z + D_SSM + D_XBC
    wi = w_in[l]
    w_pe = wi[:, o_pe:o_z]
    w_in_p = jnp.concatenate(
        [wi[:, :o_pe], w_pe, _swap_halves(w_pe), wi[:, o_z:o_dt],
         wi[:, o_dt:], jnp.zeros((D_MODEL, LANES - SSM_HEADS), F32)], axis=1).astype(BF16)

    wq = w_uq[l] * (SOFTMAX_SCALE * math.log2(math.e))
    wq_pe = wq[:, :, QK_NOPE:]
    w_q = jnp.concatenate(
        [wq[:, :, :QK_NOPE].reshape(Q_RANK, D_ATTN),
         jnp.concatenate([wq_pe, _swap_halves(wq_pe)], axis=2).reshape(Q_RANK, D_ATTN)], axis=1).astype(BF16)
    wkv = w_ukv[l]
    w_kv = jnp.concatenate([wkv[:, :, :QK_NOPE].reshape(KV_RANK, D_ATTN),
                            wkv[:, :, QK_NOPE:].reshape(KV_RANK, D_ATTN)], axis=1).astype(BF16)

    q, k, vt, z, xbc, dt = _front_call(
        x, meta_tile, norm_mix_pre[l][None], w_in_p, q_a_norm[l][None], w_q,
        kv_a_norm[l][None], w_kv, _rope_table(seq, lp))

    attn = _attn_call(q, k, vt)

    ssm = _ssd_call(z, xbc, dt, ssm_conv_w[l], ssm_conv_b[l][None], _pad_lanes(ssm_dt_bias[l]),
                    _pad_lanes(ssm_A_log[l]), jnp.repeat(ssm_D[l], SSM_P)[None], ssm_norm[l][None])

    cw = jnp.concatenate([ffn_conv_w[l], ffn_conv_b[l][None],
                          jnp.zeros((SUBLANES - FFN_CONV - 1, 2 * D_FF), F32)], axis=0)
    return _tail_call(x, attn, ssm, meta_tokens[N_META - SUBLANES:].astype(F32), attn_out_norm[l][None],
                      w_out[l].astype(BF16), norm_mix_post[l][None], norm_ffn_pre[l][None],
                      w_up[l].astype(BF16), cw, w_down[l].astype(BF16), norm_ffn_post[l][None])
```

```python
import functools
import math

import jax
import jax.numpy as jnp
import numpy as np
from jax import lax
from jax.experimental import pallas as pl
from jax.experimental.pallas import tpu as pltpu

F32 = jnp.float32
BF16 = jnp.bfloat16

D_MODEL = 1024
N_META = 16
MLA_HEADS = 8
QK_NOPE = 128
QK_ROPE = 64
V_DIM = 128
Q_RANK = 384
KV_RANK = 256
ROPE_THETA = 10000.0
SOFTMAX_SCALE = (QK_NOPE + QK_ROPE) ** -0.5
D_ATTN = MLA_HEADS * V_DIM
SSM_HEADS = 16
SSM_P = 64
SSM_GROUPS = 2
SSM_N = 128
SSM_CONV = 4
CHUNK = 128
D_SSM = SSM_HEADS * SSM_P
D_XBC = D_SSM + 2 * SSM_GROUPS * SSM_N
D_FF = 2816
FFN_CONV = 3
EPS = 1e-6

LANES = 128
SUBLANES = 8
ROW_TILE = 512
BLK = 128
QK_PAD = 256
FF_CHUNK = 256
FFN_TILE = 512
ATTN_GROUPS = 2
ATTN_UNROLL = 8
N_FF_CHUNKS = D_FF // FF_CHUNK
LAT_W = 768
IN_W = LAT_W + D_SSM + D_XBC + LANES
NEG = -1e30
VMEM_LIMIT = 56 * 1024 * 1024


def _dot(a, b):
    return jnp.dot(a, b, preferred_element_type=F32)


def _dot_nt(a, b):
    return lax.dot_general(a, b, (((1,), (1,)), ((), ())), preferred_element_type=F32)


def _rms(x, gain):
    ms = jnp.mean(x * x, axis=-1, keepdims=True)
    return x * lax.rsqrt(ms + EPS) * gain


def _silu(x):
    h = 0.5 * x
    return h + h * jnp.tanh(h)


def _const_spec(shape):
    nd = len(shape)
    return pl.BlockSpec(shape, lambda *_: (0,) * nd, pipeline_mode=pl.Buffered(1))


def _params():
    return pltpu.CompilerParams(dimension_semantics=("arbitrary", "arbitrary"),
                                vmem_limit_bytes=VMEM_LIMIT)


def _front_kernel(n_real, x_ref, meta_ref, g_ref, win_ref, qan_ref, wq_ref, kvan_ref, wkv_ref, tab_ref,
                  qt_ref, k_ref, vt_ref, z_ref, xbc_ref, dt_ref):
    i = pl.program_id(1)
    h = jnp.where(i == n_real, meta_ref[...], x_ref[0])
    hn = _rms(h, g_ref[...]).astype(BF16)
    lat = _dot(hn, win_ref[:, 0:LAT_W])
    o = LAT_W
    z_ref[0] = _dot(hn, win_ref[:, o:o + D_SSM])
    o += D_SSM
    xbc_ref[0] = _dot(hn, win_ref[:, o:o + D_XBC])
    o += D_XBC
    dt_ref[0] = _dot(hn, win_ref[:, o:o + LANES])

    tab = tab_ref[...]
    lane = lax.broadcasted_iota(jnp.int32, tab.shape, 1)

    def rope(pair):
        a = pair * tab
        return jnp.where(lane < QK_ROPE, a + pltpu.roll(a, QK_ROPE, 1), 0.0)

    qn = _rms(lat[:, 0:Q_RANK], qan_ref[...]).astype(BF16)
    q = _dot(qn, wq_ref[...])
    kvn = _rms(lat[:, Q_RANK:Q_RANK + KV_RANK], kvan_ref[...]).astype(BF16)
    kv = _dot(kvn, wkv_ref[...])
    k_pe = rope(lat[:, Q_RANK + KV_RANK:LAT_W]).astype(BF16)
    for hd in range(MLA_HEADS):
        sl = slice(hd * LANES, (hd + 1) * LANES)
        qt_ref[0, hd, 0, 0:LANES, :] = q[:, sl].T.astype(BF16)
        qt_ref[0, hd, 0, LANES:QK_PAD, :] = rope(q[:, D_ATTN + hd * LANES:D_ATTN + (hd + 1) * LANES]).T.astype(BF16)
        k_ref[0, hd, :, 0:LANES] = kv[:, sl].astype(BF16)
        k_ref[0, hd, :, LANES:QK_PAD] = k_pe
        v = kv[:, D_ATTN + hd * LANES:D_ATTN + (hd + 1) * LANES]
        vt_ref[0, hd, 0] = v.T.astype(BF16)


def _front_call(x, meta_tile, gain, w_in, q_an, w_q, kv_an, w_kv, rope_tab):
    bsz, seq, _ = x.shape
    n_real = seq // ROW_TILE
    n_tiles = n_real + 1
    lp = n_tiles * ROW_TILE
    row = lambda w: pl.BlockSpec((1, ROW_TILE, w), lambda b, i: (b, i, 0))
    tab = pl.BlockSpec((ROW_TILE, LANES), lambda b, i: (i, 0))
    qk_spec = pl.BlockSpec((1, MLA_HEADS, ROW_TILE, QK_PAD), lambda b, i: (b, 0, i, 0))
    return pl.pallas_call(
        functools.partial(_front_kernel, n_real),
        grid=(bsz, n_tiles),
        in_specs=[
            pl.BlockSpec((1, ROW_TILE, D_MODEL), lambda b, i: (b, jnp.minimum(i, n_real - 1), 0)),
            _const_spec((ROW_TILE, D_MODEL)),
            _const_spec((1, D_MODEL)),
            _const_spec((D_MODEL, IN_W)),
            _const_spec((1, Q_RANK)),
            _const_spec((Q_RANK, 2 * D_ATTN)),
            _const_spec((1, KV_RANK)),
            _const_spec((KV_RANK, 2 * D_ATTN)),
            tab,
        ],
        out_specs=[
            pl.BlockSpec((1, MLA_HEADS, 1, QK_PAD, ROW_TILE), lambda b, i: (b, 0, i, 0, 0)),
            qk_spec,
            pl.BlockSpec((1, MLA_HEADS, 1, V_DIM, ROW_TILE), lambda b, i: (b, 0, i, 0, 0)),
            row(D_SSM), row(D_XBC), row(LANES),
        ],
        out_shape=[
            jax.ShapeDtypeStruct((bsz, MLA_HEADS, n_tiles, QK_PAD, ROW_TILE), BF16),
            jax.ShapeDtypeStruct((bsz, MLA_HEADS, lp, QK_PAD), BF16),
            jax.ShapeDtypeStruct((bsz, MLA_HEADS, n_tiles, V_DIM, ROW_TILE), BF16),
            jax.ShapeDtypeStruct((bsz, lp, D_SSM), F32),
            jax.ShapeDtypeStruct((bsz, lp, D_XBC), F32),
            jax.ShapeDtypeStruct((bsz, lp, LANES), F32),
        ],
        compiler_params=_params(),
        name="front",
    )(x, meta_tile, gain, w_in, q_an, w_q, kv_an, w_kv, rope_tab)


def _attn_kernel(n_real, qt_ref, k_ref, vt_ref, o_ref, s0_ref, s1_ref, c0_ref, c1_ref, m_ref, l_ref, acc_ref):
    pre = n_real * ROW_TILE
    s_refs, c_refs = (s0_ref, s1_ref), (c0_ref, c1_ref)
    k_meta = k_ref[0, 0, pre:pre + N_META, :]
    vt_meta = vt_ref[0, 0, n_real, :, 0:N_META]

    def softmax_cols(s):
        m = jnp.max(s, axis=0, keepdims=True)
        p = jnp.exp2(s - m)
        return m, jnp.sum(p, axis=0, keepdims=True), p

    s = _dot(k_meta, qt_ref[0, 0, n_real, :, 0:BLK])
    kk = lax.broadcasted_iota(jnp.int32, s.shape, 0)
    qq = lax.broadcasted_iota(jnp.int32, s.shape, 1)
    _, l, p = softmax_cols(jnp.where(kk <= qq, s, NEG))
    o_ref[0, pre:pre + BLK, :] = (_dot(vt_meta, p.astype(BF16)) / l).T
    o_ref[0, pre + BLK:, :] = jnp.zeros((ROW_TILE - BLK, V_DIM), F32)

    for qi in range(n_real):
        m, l, p = softmax_cols(_dot(k_meta, qt_ref[0, 0, qi]))
        m_ref[qi] = m
        l_ref[qi] = l
        acc_ref[qi] = _dot(vt_meta, p.astype(BF16))

    def scores(qi, j):
        ks = pl.multiple_of(j * ROW_TILE, ROW_TILE)
        return _dot(k_ref[0, 0, pl.ds(ks, ROW_TILE), :], qt_ref[0, 0, qi])

    def run(n_steps, first, advance, diagonal):
        def produce(slot, qi, j):
            s = scores(jnp.minimum(qi, n_real - 1), jnp.minimum(j, n_real - 1))
            if diagonal:
                kd = lax.broadcasted_iota(jnp.int32, s.shape, 0)
                qd = lax.broadcasted_iota(jnp.int32, s.shape, 1)
                s = jnp.where(kd <= qd, s, NEG)
            s_refs[slot][...] = s
            c_refs[slot][...] = jnp.max(s, axis=0, keepdims=True)

        def step(slot, qi, j):
            nqi, nj = advance(qi, j)
            produce(1 - slot, nqi, nj)
            for g in range(ATTN_GROUPS):
                cols = slice(g * (ROW_TILE // ATTN_GROUPS), (g + 1) * (ROW_TILE // ATTN_GROUPS))
                m_prev = m_ref[qi, :, cols]
                m_new = jnp.maximum(m_prev, c_refs[slot][:, cols])
                alpha = jnp.exp2(m_prev - m_new)
                p = jnp.exp2(s_refs[slot][:, cols] - m_new)
                l_ref[qi, :, cols] = alpha * l_ref[qi, :, cols] + jnp.sum(p, axis=0, keepdims=True)
                m_ref[qi, :, cols] = m_new
                acc_ref[qi, :, cols] = alpha * acc_ref[qi, :, cols] + _dot(vt_ref[0, 0, j], p.astype(BF16))
            return nqi, nj

        def body(_, carry):
            for u in range(ATTN_UNROLL):
                carry = step(u % 2, *carry)
            return carry

        if n_steps > 0:
            first = (jnp.int32(first[0]), jnp.int32(first[1]))
            produce(0, *first)
            carry = lax.fori_loop(0, n_steps // ATTN_UNROLL, body, first)
            for u in range(n_steps % ATTN_UNROLL):
                carry = step(u % 2, *carry)

    run(n_real, (0, 0), lambda qi, j: (qi + 1, j + 1), True)

    def below(qi, j):
        wrap = j + 1 == qi
        return jnp.where(wrap, qi + 1, qi), jnp.where(wrap, 0, j + 1)

    run(n_real * (n_real - 1) // 2, (1, 0), below, False)

    def finish_tile(qi, c):
        qs = pl.multiple_of(qi * ROW_TILE, ROW_TILE)
        o_ref[0, pl.ds(qs, ROW_TILE), :] = (acc_ref[qi] / l_ref[qi]).T
        return c

    lax.fori_loop(0, n_real, finish_tile, 0)


def _attn_call(qt, k, vt):
    bsz, _, lp, _ = k.shape
    n_real = lp // ROW_TILE - 1
    tiles = lambda rows: pl.BlockSpec((1, 1, n_real + 1, rows, ROW_TILE), lambda b, h: (b, h, 0, 0, 0))
    return pl.pallas_call(
        functools.partial(_attn_kernel, n_real),
        grid=(bsz, MLA_HEADS),
        in_specs=[tiles(QK_PAD), pl.BlockSpec((1, 1, lp, QK_PAD), lambda b, h: (b, h, 0, 0)), tiles(V_DIM)],
        out_specs=pl.BlockSpec((1, lp, V_DIM), lambda b, h: (b, 0, h)),
        out_shape=jax.ShapeDtypeStruct((bsz, lp, D_ATTN), F32),
        scratch_shapes=[
            pltpu.VMEM((ROW_TILE, ROW_TILE), F32), pltpu.VMEM((ROW_TILE, ROW_TILE), F32),
            pltpu.VMEM((1, ROW_TILE), F32), pltpu.VMEM((1, ROW_TILE), F32),
            pltpu.VMEM((n_real, 1, ROW_TILE), F32), pltpu.VMEM((n_real, 1, ROW_TILE), F32),
            pltpu.VMEM((n_real, V_DIM, ROW_TILE), F32),
        ],
        compiler_params=_params(),
        name="attn",
    )(qt, k, vt)


N_PAIRS = SSM_HEADS // 2


def _split3(a):
    a1 = a.astype(BF16)
    r1 = a - a1.astype(F32)
    a2 = r1.astype(BF16)
    a3 = (r1 - a2.astype(F32)).astype(BF16)
    return a1, a2, a3


def _ssd_kernel(n_chunks, z_ref, xbc_ref, dt_ref, cw_ref, cb_ref, dtb_ref, alog_ref, dskip_ref, ng_ref,
                y_ref, xbuf_ref, state_ref):
    c = pl.program_id(1)
    bsz = z_ref.shape[0]

    @pl.when(c == 0)
    def _():
        xbuf_ref[:, 0:SUBLANES, :] = jnp.zeros((bsz, SUBLANES, D_XBC), F32)
        state_ref[...] = jnp.zeros(state_ref.shape, F32)

    @pl.when(c >= n_chunks)
    def _():
        y_ref[...] = jnp.zeros(y_ref.shape, F32)

    def chunk(b):
        x = xbc_ref[b]
        xbuf_ref[b, SUBLANES:SUBLANES + CHUNK, :] = x
        conv = cb_ref[...] + cw_ref[SSM_CONV - 1:SSM_CONV, :] * x
        for d in range(1, SSM_CONV):
            conv = conv + (cw_ref[SSM_CONV - 1 - d:SSM_CONV - d, :]
                           * xbuf_ref[b, SUBLANES - d:SUBLANES - d + CHUNK, :])
        xbuf_ref[b, 0:SUBLANES, :] = jnp.where(c == 0, x[N_META - SUBLANES:N_META, :], x[CHUNK - SUBLANES:CHUNK, :])
        xc = _silu(conv)
        xs = xc[:, 0:D_SSM]

        row = lax.broadcasted_iota(jnp.int32, (CHUNK, LANES), 0)
        col = lax.broadcasted_iota(jnp.int32, (CHUNK, LANES), 1)
        dt = jax.nn.softplus(dt_ref[b] + dtb_ref[...])
        dt = jnp.where((c > 0) | (row < N_META), dt, 0.0)
        a = dt * (-jnp.exp(alog_ref[...]))
        tri = (col <= row).astype(BF16)
        a1, a2, a3 = _split3(a)
        acs = _dot(tri, a1) + _dot(tri, a2) + _dot(tri, a3)
        acs_t = acs.T
        dt_t = dt.T
        e_acs = jnp.exp(acs)
        last_t = acs_t[:, CHUNK - 1:CHUNK]
        w_t = dt_t * jnp.exp(last_t - acs_t)
        dec_t = jnp.exp(last_t)
        causal = col <= row
        left = col < SSM_P

        for g in range(SSM_GROUPS):
            bm = xc[:, D_SSM + g * SSM_N:D_SSM + (g + 1) * SSM_N]
            cm = xc[:, D_SSM + SSM_GROUPS * SSM_N + g * SSM_N:D_SSM + SSM_GROUPS * SSM_N + (g + 1) * SSM_N]
            bm16 = bm.astype(BF16)
            cm16 = cm.astype(BF16)
            cb = _dot_nt(cm16, bm16)
            bm_t = bm.T
            for pp in range(N_PAIRS // SSM_GROUPS):
                pair = g * (N_PAIRS // SSM_GROUPS) + pp
                xs_pair = xs[:, pair * LANES:(pair + 1) * LANES]
                prev = state_ref[b, pair]
                rhs_x = [jnp.where(left, xs_pair, 0.0).astype(BF16), jnp.where(left, 0.0, xs_pair).astype(BF16)]
                rhs_s = [jnp.where(left, prev, 0.0).astype(BF16), jnp.where(left, 0.0, prev).astype(BF16)]
                lhs_m, lhs_c, lhs_b, decs = [], [], [], []
                for k in range(2):
                    hd = 2 * pair + k
                    seg = acs[:, hd:hd + 1] - acs_t[hd:hd + 1, :]
                    lm = jnp.exp(jnp.where(causal, seg, NEG))
                    lhs_m.append((cb * lm * dt_t[hd:hd + 1, :]).astype(BF16))
                    lhs_c.append((cm * e_acs[:, hd:hd + 1]).astype(BF16))
                    lhs_b.append((bm_t * w_t[hd:hd + 1, :]).astype(BF16))
                    decs.append(jnp.broadcast_to(dec_t[hd:hd + 1, :], (SSM_N, LANES)))
                y_pair = _dot(jnp.concatenate(lhs_m + lhs_c, axis=1), jnp.concatenate(rhs_x + rhs_s, axis=0))
                new = _dot(jnp.concatenate(lhs_b, axis=1), jnp.concatenate(rhs_x, axis=0))
                state_ref[b, pair] = jnp.where(left, decs[0], decs[1]) * prev + new
                y_ref[b, :, pair * LANES:(pair + 1) * LANES] = y_pair

        y = (y_ref[b] + dskip_ref[...] * xs) * _silu(z_ref[b])
        gsz = D_SSM // SSM_GROUPS
        for g in range(SSM_GROUPS):
            yg = y[:, g * gsz:(g + 1) * gsz]
            ms = jnp.mean(yg * yg, axis=-1, keepdims=True)
            y_ref[b, :, g * gsz:(g + 1) * gsz] = yg * lax.rsqrt(ms + EPS) * ng_ref[:, g * gsz:(g + 1) * gsz]

    @pl.when(c < n_chunks)
    def _():
        for b in range(bsz):
            chunk(b)


def _ssd_call(z, xbc, dt, conv_w, conv_b, dt_bias, a_log, d_skip, norm_gain):
    bsz, lp, _ = z.shape
    n_blocks = lp // CHUNK
    pre_blk = (lp - ROW_TILE) // CHUNK
    n_chunks = pre_blk + 1

    def blk(_, c):
        return (0, jnp.where(c == 0, pre_blk, jnp.where(c < n_chunks, c - 1, c)), 0)

    row = lambda w: pl.BlockSpec((bsz, CHUNK, w), blk)
    return pl.pallas_call(
        functools.partial(_ssd_kernel, n_chunks),
        grid=(1, n_blocks),
        in_specs=[row(D_SSM), row(D_XBC), row(LANES),
                  _const_spec((SSM_CONV, D_XBC)), _const_spec((1, D_XBC)), _const_spec((1, LANES)),
                  _const_spec((1, LANES)), _const_spec((1, D_SSM)), _const_spec((1, D_SSM))],
        out_specs=row(D_SSM),
        out_shape=jax.ShapeDtypeStruct((bsz, lp, D_SSM), F32),
        scratch_shapes=[pltpu.VMEM((bsz, SUBLANES + CHUNK, D_XBC), F32),
                        pltpu.VMEM((bsz, N_PAIRS, SSM_N, LANES), F32)],
        compiler_params=_params(),
        name="ssd",
    )(z, xbc, dt, conv_w, conv_b, dt_bias, a_log, d_skip, norm_gain)


def _shifted(u, tail, d):
    r = pltpu.roll(u, d, 0)
    row = lax.broadcasted_iota(jnp.int32, (SUBLANES, u.shape[1]), 0)
    head = jnp.where(row < d, pltpu.roll(tail, d, 0), r[0:SUBLANES])
    return jnp.concatenate([head, r[SUBLANES:]], axis=0)


def _tail_kernel(x_ref, attn_ref, ssm_ref, xm_ref, attnm_ref, ssmm_ref, ga_ref, wout_ref, gmix_ref,
                 gpre_ref, wu_ref, cw_ref, wd_ref, gpost_ref, out_ref, tail_ref):
    i = pl.program_id(1)
    g_pre = gpre_ref[...]
    cols = lambda c, half: slice(half * D_FF + c * FF_CHUNK, half * D_FF + (c + 1) * FF_CHUNK)

    def mixed(h, attn, ssm):
        an = _rms(attn, ga_ref[...]).astype(BF16)
        mix = _dot(an, wout_ref[0:D_ATTN, :]) + _dot(ssm.astype(BF16), wout_ref[D_ATTN:, :])
        return h + _rms(mix, gmix_ref[...])

    @pl.when(i == 0)
    def _():
        rows = slice(N_META - SUBLANES, N_META)
        hm = _rms(mixed(xm_ref[...], attnm_ref[0, rows, :], ssmm_ref[0, rows, :]), g_pre).astype(BF16)
        for c in range(2 * N_FF_CHUNKS):
            sl = slice(c * FF_CHUNK, (c + 1) * FF_CHUNK)
            tail_ref[:, sl] = _dot(hm, wu_ref[:, sl])

    half_rows = FFN_TILE // 2
    halves = [slice(k * half_rows, (k + 1) * half_rows) for k in range(2)]
    h1 = [mixed(x_ref[0, hs, :], attn_ref[0, hs, :], ssm_ref[0, hs, :]) for hs in halves]
    hn = [_rms(h, g_pre).astype(BF16) for h in h1]

    def conv(u, sl):
        tails = [tail_ref[:, sl], u[0][half_rows - SUBLANES:half_rows, :]]
        tail_ref[:, sl] = u[1][half_rows - SUBLANES:half_rows, :]
        return [cw_ref[FFN_CONV:FFN_CONV + 1, sl] + cw_ref[2:3, sl] * uk + cw_ref[1:2, sl] * _shifted(uk, tk, 1)
                + cw_ref[0:1, sl] * _shifted(uk, tk, 2) for uk, tk in zip(u, tails)]

    def up(c):
        return [[_dot(hn[k], wu_ref[:, cols(c, part)]) for k in range(2)] for part in range(2)]

    def gate(c, u):
        g, v = conv(u[0], cols(c, 0)), conv(u[1], cols(c, 1))
        return [(_silu(g[k]) * v[k]).astype(BF16) for k in range(2)]

    u = {0: up(0), 1: up(1)}
    act = gate(0, u.pop(0))
    down = [None, None]
    for c in range(N_FF_CHUNKS):
        if c + 2 < N_FF_CHUNKS:
            u[c + 2] = up(c + 2)
        for k in range(2):
            d = _dot(act[k], wd_ref[c * FF_CHUNK:(c + 1) * FF_CHUNK, :])
            down[k] = d if down[k] is None else down[k] + d
        if c + 1 < N_FF_CHUNKS:
            act = gate(c + 1, u.pop(c + 1))
    for k in range(2):
        out_ref[0, halves[k], :] = h1[k] + _rms(down[k], gpost_ref[...])


def _tail_call(x, attn, ssm, meta_rows, gain_attn, w_out, gain_mix, gain_pre, w_up, conv_wb, w_down, gain_post):
    bsz, seq, _ = x.shape
    row = pl.BlockSpec((1, FFN_TILE, D_MODEL), lambda b, i: (b, i, 0))
    prefix = pl.BlockSpec((1, BLK, D_MODEL), lambda b, i: (b, seq // BLK, 0))
    vec = _const_spec((1, D_MODEL))
    return pl.pallas_call(
        _tail_kernel,
        grid=(bsz, seq // FFN_TILE),
        in_specs=[row, row, row, _const_spec((SUBLANES, D_MODEL)), prefix, prefix,
                  vec, _const_spec((D_ATTN + D_SSM, D_MODEL)), vec, vec,
                  _const_spec((D_MODEL, 2 * D_FF)), _const_spec((SUBLANES, 2 * D_FF)),
                  _const_spec((D_FF, D_MODEL)), vec],
        out_specs=row,
        out_shape=jax.ShapeDtypeStruct((bsz, seq, D_MODEL), F32),
        scratch_shapes=[pltpu.VMEM((SUBLANES, 2 * D_FF), F32)],
        compiler_params=_params(),
        name="tail",
    )(x, attn, ssm, meta_rows, attn, ssm, gain_attn, w_out, gain_mix, gain_pre, w_up, conv_wb, w_down, gain_post)


def _swap_halves(t):
    half = t.shape[-1] // 2
    return jnp.concatenate([t[..., half:], t[..., :half]], axis=-1)


def _rope_table(seq, lp):
    inv = ROPE_THETA ** (-jnp.arange(0, QK_ROPE, 2, dtype=F32) / QK_ROPE)

    def cs(pos):
        ang = pos.astype(F32)[:, None] * inv[None, :]
        return jnp.cos(ang), jnp.sin(ang)

    c_off, s_off = cs(N_META + ROW_TILE * jnp.arange(seq // ROW_TILE))
    c_row, s_row = cs(jnp.arange(ROW_TILE))
    cos = (c_off[:, None] * c_row[None] - s_off[:, None] * s_row[None]).reshape(seq, -1)
    sin = (s_off[:, None] * c_row[None] + c_off[:, None] * s_row[None]).reshape(seq, -1)
    c_meta, s_meta = cs(jnp.arange(N_META))
    tab = lambda c, s: jnp.concatenate([c, c, -s, s], axis=1)
    return jnp.concatenate([tab(cos, sin), tab(c_meta, s_meta),
                            jnp.zeros((lp - seq - N_META, LANES), F32)], axis=0)


def _pad_lanes(v, width=LANES):
    return jnp.zeros((1, width), F32).at[0, :v.shape[0]].set(v)


def kernel(x, meta_tokens, norm_mix_pre, norm_mix_post, norm_ffn_pre, norm_ffn_post, w_in, q_a_norm, w_uq,
           kv_a_norm, w_ukv, attn_out_norm, ssm_conv_w, ssm_conv_b, ssm_dt_bias, ssm_A_log, ssm_D, ssm_norm,
           w_out, w_up, ffn_conv_w, ffn_conv_b, w_down):
    bsz, seq, _ = x.shape
    lp = seq + ROW_TILE
    l = 0
    meta_tile = jnp.zeros((ROW_TILE, D_MODEL), F32).at[0:N_META].set(meta_tokens.astype(F32))

    o_kv, o_pe, o_z = Q_RANK, Q_RANK + KV_RANK, Q_RANK + KV_RANK + QK_ROPE
    o_xbc, o_dt = o_z + D_SSM, o_z + D_SSM + D_XBC
    wi = w_in[l]
    w_pe = wi[:, o_pe:o_z]
    w_in_p = jnp.concatenate(
        [wi[:, :o_pe], w_pe, _swap_halves(w_pe), wi[:, o_z:o_dt],
         wi[:, o_dt:], jnp.zeros((D_MODEL, LANES - SSM_HEADS), F32)], axis=1).astype(BF16)

    wq = w_uq[l] * (SOFTMAX_SCALE * math.log2(math.e))
    wq_pe = wq[:, :, QK_NOPE:]
    w_q = jnp.concatenate(
        [wq[:, :, :QK_NOPE].reshape(Q_RANK, D_ATTN),
         jnp.concatenate([wq_pe, _swap_halves(wq_pe)], axis=2).reshape(Q_RANK, D_ATTN)], axis=1).astype(BF16)
    wkv = w_ukv[l]
    w_kv = jnp.concatenate([wkv[:, :, :QK_NOPE].reshape(KV_RANK, D_ATTN),
                            wkv[:, :, QK_NOPE:].reshape(KV_RANK, D_ATTN)], axis=1).astype(BF16)

    q, k, vt, z, xbc, dt = _front_call(
        x, meta_tile, norm_mix_pre[l][None], w_in_p, q_a_norm[l][None], w_q,
        kv_a_norm[l][None], w_kv, _rope_table(seq, lp))

    attn = _attn_call(q, k, vt)

    ssm = _ssd_call(z, xbc, dt, ssm_conv_w[l], ssm_conv_b[l][None], _pad_lanes(ssm_dt_bias[l]),
                    _pad_lanes(ssm_A_log[l]), jnp.repeat(ssm_D[l], SSM_P)[None], ssm_norm[l][None])

    cw = jnp.concatenate([ffn_conv_w[l], ffn_conv_b[l][None],
                          jnp.zeros((SUBLANES - FFN_CONV - 1, 2 * D_FF), F32)], axis=0)
    return _tail_call(x, attn, ssm, meta_tokens[N_META - SUBLANES:].astype(F32), attn_out_norm[l][None],
                      w_out[l].astype(BF16), norm_mix_post[l][None], norm_ffn_pre[l][None],
                      w_up[l].astype(BF16), cw, w_down[l].astype(BF16), norm_ffn_post[l][None])
```

```python
import functools
import math

import jax
import jax.numpy as jnp
import numpy as np
from jax import lax
from jax.experimental import pallas as pl
from jax.experimental.pallas import tpu as pltpu

F32 = jnp.float32
BF16 = jnp.bfloat16

D_MODEL = 1024
N_META = 16
MLA_HEADS = 8
QK_NOPE = 128
QK_ROPE = 64
V_DIM = 128
Q_RANK = 384
KV_RANK = 256
ROPE_THETA = 10000.0
SOFTMAX_SCALE = (QK_NOPE + QK_ROPE) ** -0.5
D_ATTN = MLA_HEADS * V_DIM
SSM_HEADS = 16
SSM_P = 64
SSM_GROUPS = 2
SSM_N = 128
SSM_CONV = 4
CHUNK = 128
D_SSM = SSM_HEADS * SSM_P
D_XBC = D_SSM + 2 * SSM_GROUPS * SSM_N
D_FF = 2816
FFN_CONV = 3
EPS = 1e-6

LANES = 128
SUBLANES = 8
ROW_TILE = 512
BLK = 128
QK_PAD = 256
FF_CHUNK = 256
FFN_TILE = 512
ATTN_UNROLL = 8
N_FF_CHUNKS = D_FF // FF_CHUNK
LAT_W = 768
IN_W = LAT_W + D_SSM + D_XBC + LANES
NEG = -1e30
VMEM_LIMIT = 56 * 1024 * 1024


def _dot(a, b):
    return jnp.dot(a, b, preferred_element_type=F32)


def _dot_nt(a, b):
    return lax.dot_general(a, b, (((1,), (1,)), ((), ())), preferred_element_type=F32)


def _rms(x, gain):
    ms = jnp.mean(x * x, axis=-1, keepdims=True)
    return x * lax.rsqrt(ms + EPS) * gain


def _silu(x):
    h = 0.5 * x
    return h + h * jnp.tanh(h)


def _const_spec(shape):
    nd = len(shape)
    return pl.BlockSpec(shape, lambda *_: (0,) * nd, pipeline_mode=pl.Buffered(1))


def _params():
    return pltpu.CompilerParams(dimension_semantics=("arbitrary", "arbitrary"),
                                vmem_limit_bytes=VMEM_LIMIT)


def _front_kernel(n_real, x_ref, meta_ref, g_ref, win_ref, qan_ref, wq_ref, kvan_ref, wkv_ref, tab_ref,
                  qt_ref, k_ref, vt_ref, z_ref, xbc_ref, dt_ref):
    i = pl.program_id(1)
    h = jnp.where(i == n_real, meta_ref[...], x_ref[0])
    hn = _rms(h, g_ref[...]).astype(BF16)
    lat = _dot(hn, win_ref[:, 0:LAT_W])
    o = LAT_W
    z_ref[0] = _dot(hn, win_ref[:, o:o + D_SSM])
    o += D_SSM
    xbc_ref[0] = _dot(hn, win_ref[:, o:o + D_XBC])
    o += D_XBC
    dt_ref[0] = _dot(hn, win_ref[:, o:o + LANES])

    tab = tab_ref[...]
    lane = lax.broadcasted_iota(jnp.int32, tab.shape, 1)

    def rope(pair):
        a = pair * tab
        return jnp.where(lane < QK_ROPE, a + pltpu.roll(a, QK_ROPE, 1), 0.0)

    qn = _rms(lat[:, 0:Q_RANK], qan_ref[...]).astype(BF16)
    q = _dot(qn, wq_ref[...])
    kvn = _rms(lat[:, Q_RANK:Q_RANK + KV_RANK], kvan_ref[...]).astype(BF16)
    kv = _dot(kvn, wkv_ref[...])
    k_pe = rope(lat[:, Q_RANK + KV_RANK:LAT_W]).astype(BF16)
    for hd in range(MLA_HEADS):
        sl = slice(hd * LANES, (hd + 1) * LANES)
        qt_ref[0, hd, 0, 0:LANES, :] = q[:, sl].T.astype(BF16)
        qt_ref[0, hd, 0, LANES:QK_PAD, :] = rope(q[:, D_ATTN + hd * LANES:D_ATTN + (hd + 1) * LANES]).T.astype(BF16)
        k_ref[0, hd, :, 0:LANES] = kv[:, sl].astype(BF16)
        k_ref[0, hd, :, LANES:QK_PAD] = k_pe
        v = kv[:, D_ATTN + hd * LANES:D_ATTN + (hd + 1) * LANES]
        vt_ref[0, hd, 0] = v.T.astype(BF16)


def _front_call(x, meta_tile, gain, w_in, q_an, w_q, kv_an, w_kv, rope_tab):
    bsz, seq, _ = x.shape
    n_real = seq // ROW_TILE
    n_tiles = n_real + 1
    lp = n_tiles * ROW_TILE
    row = lambda w: pl.BlockSpec((1, ROW_TILE, w), lambda b, i: (b, i, 0))
    tab = pl.BlockSpec((ROW_TILE, LANES), lambda b, i: (i, 0))
    qk_spec = pl.BlockSpec((1, MLA_HEADS, ROW_TILE, QK_PAD), lambda b, i: (b, 0, i, 0))
    return pl.pallas_call(
        functools.partial(_front_kernel, n_real),
        grid=(bsz, n_tiles),
        in_specs=[
            pl.BlockSpec((1, ROW_TILE, D_MODEL), lambda b, i: (b, jnp.minimum(i, n_real - 1), 0)),
            _const_spec((ROW_TILE, D_MODEL)),
            _const_spec((1, D_MODEL)),
            _const_spec((D_MODEL, IN_W)),
            _const_spec((1, Q_RANK)),
            _const_spec((Q_RANK, 2 * D_ATTN)),
            _const_spec((1, KV_RANK)),
            _const_spec((KV_RANK, 2 * D_ATTN)),
            tab,
        ],
        out_specs=[
            pl.BlockSpec((1, MLA_HEADS, 1, QK_PAD, ROW_TILE), lambda b, i: (b, 0, i, 0, 0)),
            qk_spec,
            pl.BlockSpec((1, MLA_HEADS, 1, V_DIM, ROW_TILE), lambda b, i: (b, 0, i, 0, 0)),
            row(D_SSM), row(D_XBC), row(LANES),
        ],
        out_shape=[
            jax.ShapeDtypeStruct((bsz, MLA_HEADS, n_tiles, QK_PAD, ROW_TILE), BF16),
            jax.ShapeDtypeStruct((bsz, MLA_HEADS, lp, QK_PAD), BF16),
            jax.ShapeDtypeStruct((bsz, MLA_HEADS, n_tiles, V_DIM, ROW_TILE), BF16),
            jax.ShapeDtypeStruct((bsz, lp, D_SSM), F32),
            jax.ShapeDtypeStruct((bsz, lp, D_XBC), F32),
            jax.ShapeDtypeStruct((bsz, lp, LANES), F32),
        ],
        compiler_params=_params(),
        name="front",
    )(x, meta_tile, gain, w_in, q_an, w_q, kv_an, w_kv, rope_tab)


def _attn_kernel(n_real, qt_ref, k_ref, vt_ref, o_ref, s0_ref, s1_ref, sm0_ref, sm1_ref, c0_ref, c1_ref,
                 m_ref, l_ref, acc_ref):
    pre = n_real * ROW_TILE
    s_refs, sm_refs, c_refs = (s0_ref, s1_ref), (sm0_ref, sm1_ref), (c0_ref, c1_ref)
    k_meta = k_ref[0, 0, pre:pre + N_META, :]
    vt_meta = vt_ref[0, 0, n_real, :, 0:N_META]

    def softmax_cols(s):
        m = jnp.max(s, axis=0, keepdims=True)
        p = jnp.exp2(s - m)
        return m, jnp.sum(p, axis=0, keepdims=True), p

    s = _dot(k_meta, qt_ref[0, 0, n_real, :, 0:BLK])
    kk = lax.broadcasted_iota(jnp.int32, s.shape, 0)
    qq = lax.broadcasted_iota(jnp.int32, s.shape, 1)
    _, l, p = softmax_cols(jnp.where(kk <= qq, s, NEG))
    o_ref[0, pre:pre + BLK, :] = (_dot(vt_meta, p.astype(BF16)) / l).T
    o_ref[0, pre + BLK:, :] = jnp.zeros((ROW_TILE - BLK, V_DIM), F32)

    def run(n_steps, first, advance, diagonal):
        def produce(slot, qi, j):
            qt = qt_ref[0, 0, jnp.minimum(qi, n_real - 1)]
            ks = pl.multiple_of(jnp.minimum(j, n_real - 1) * ROW_TILE, ROW_TILE)
            s = _dot(k_ref[0, 0, pl.ds(ks, ROW_TILE), :], qt)
            c = None
            if diagonal:
                kd = lax.broadcasted_iota(jnp.int32, s.shape, 0)
                qd = lax.broadcasted_iota(jnp.int32, s.shape, 1)
                s = jnp.where(kd <= qd, s, NEG)
                s_meta = _dot(k_meta, qt)
                sm_refs[slot][...] = s_meta
                c = jnp.max(s_meta, axis=0, keepdims=True)
            s_refs[slot][...] = s
            c_tile = jnp.max(s, axis=0, keepdims=True)
            c_refs[slot][...] = c_tile if c is None else jnp.maximum(c, c_tile)

        def step(slot, qi, j):
            nqi, nj = advance(qi, j)
            produce(1 - slot, nqi, nj)
            m_prev = m_ref[qi]
            m_new = jnp.maximum(m_prev, c_refs[slot][...])
            alpha = jnp.exp2(m_prev - m_new)
            p = jnp.exp2(s_refs[slot][...] - m_new)
            l = alpha * l_ref[qi] + jnp.sum(p, axis=0, keepdims=True)
            acc = alpha * acc_ref[qi] + _dot(vt_ref[0, 0, j], p.astype(BF16))
            if diagonal:
                p_meta = jnp.exp2(sm_refs[slot][...] - m_new)
                l = l + jnp.sum(p_meta, axis=0, keepdims=True)
                acc = acc + _dot(vt_meta, p_meta.astype(BF16))
                qs = pl.multiple_of(qi * ROW_TILE, ROW_TILE)
                o_ref[0, pl.ds(qs, ROW_TILE), :] = (acc / l).T
            else:
                m_ref[qi] = m_new
                l_ref[qi] = l
                acc_ref[qi] = acc
            return nqi, nj

        def body(_, carry):
            for u in range(ATTN_UNROLL):
                carry = step(u % 2, *carry)
            return carry

        if n_steps > 0:
            first = (jnp.int32(first[0]), jnp.int32(first[1]))
            produce(0, *first)
            carry = lax.fori_loop(0, n_steps // ATTN_UNROLL, body, first)
            for u in range(n_steps % ATTN_UNROLL):
                carry = step(u % 2, *carry)

    m_ref[...] = jnp.full(m_ref.shape, NEG, F32)
    l_ref[...] = jnp.zeros(l_ref.shape, F32)
    acc_ref[...] = jnp.zeros(acc_ref.shape, F32)

    def below(qi, j):
        wrap = j + 1 == qi
        return jnp.where(wrap, qi + 1, qi), jnp.where(wrap, 0, j + 1)

    run(n_real * (n_real - 1) // 2, (1, 0), below, False)
    run(n_real, (0, 0), lambda qi, j: (qi + 1, j + 1), True)


def _attn_call(qt, k, vt):
    bsz, _, lp, _ = k.shape
    n_real = lp // ROW_TILE - 1
    tiles = lambda rows: pl.BlockSpec((1, 1, n_real + 1, rows, ROW_TILE), lambda b, h: (b, h, 0, 0, 0))
    return pl.pallas_call(
        functools.partial(_attn_kernel, n_real),
        grid=(bsz, MLA_HEADS),
        in_specs=[tiles(QK_PAD), pl.BlockSpec((1, 1, lp, QK_PAD), lambda b, h: (b, h, 0, 0)), tiles(V_DIM)],
        out_specs=pl.BlockSpec((1, lp, V_DIM), lambda b, h: (b, 0, h)),
        out_shape=jax.ShapeDtypeStruct((bsz, lp, D_ATTN), F32),
        scratch_shapes=[
            pltpu.VMEM((ROW_TILE, ROW_TILE), F32), pltpu.VMEM((ROW_TILE, ROW_TILE), F32),
            pltpu.VMEM((N_META, ROW_TILE), F32), pltpu.VMEM((N_META, ROW_TILE), F32),
            pltpu.VMEM((1, ROW_TILE), F32), pltpu.VMEM((1, ROW_TILE), F32),
            pltpu.VMEM((n_real, 1, ROW_TILE), F32), pltpu.VMEM((n_real, 1, ROW_TILE), F32),
            pltpu.VMEM((n_real, V_DIM, ROW_TILE), F32),
        ],
        compiler_params=_params(),
        name="attn",
    )(qt, k, vt)


N_PAIRS = SSM_HEADS // 2


def _split3(a):
    a1 = a.astype(BF16)
    r1 = a - a1.astype(F32)
    a2 = r1.astype(BF16)
    a3 = (r1 - a2.astype(F32)).astype(BF16)
    return a1, a2, a3


def _ssd_kernel(n_chunks, z_ref, xbc_ref, dt_ref, cw_ref, cb_ref, dtb_ref, alog_ref, dskip_ref, ng_ref,
                y_ref, xbuf_ref, state_ref):
    c = pl.program_id(1)
    bsz = z_ref.shape[0]

    @pl.when(c == 0)
    def _():
        xbuf_ref[:, 0:SUBLANES, :] = jnp.zeros((bsz, SUBLANES, D_XBC), F32)
        state_ref[...] = jnp.zeros(state_ref.shape, F32)

    @pl.when(c >= n_chunks)
    def _():
        y_ref[...] = jnp.zeros(y_ref.shape, F32)

    def chunk(b):
        x = xbc_ref[b]
        xbuf_ref[b, SUBLANES:SUBLANES + CHUNK, :] = x
        conv = cb_ref[...] + cw_ref[SSM_CONV - 1:SSM_CONV, :] * x
        for d in range(1, SSM_CONV):
            conv = conv + (cw_ref[SSM_CONV - 1 - d:SSM_CONV - d, :]
                           * xbuf_ref[b, SUBLANES - d:SUBLANES - d + CHUNK, :])
        xbuf_ref[b, 0:SUBLANES, :] = jnp.where(c == 0, x[N_META - SUBLANES:N_META, :], x[CHUNK - SUBLANES:CHUNK, :])
        xc = _silu(conv)
        xs = xc[:, 0:D_SSM]

        row = lax.broadcasted_iota(jnp.int32, (CHUNK, LANES), 0)
        col = lax.broadcasted_iota(jnp.int32, (CHUNK, LANES), 1)
        dt = jax.nn.softplus(dt_ref[b] + dtb_ref[...])
        dt = jnp.where((c > 0) | (row < N_META), dt, 0.0)
        a = dt * (-jnp.exp(alog_ref[...]))
        tri = (col <= row).astype(BF16)
        a1, a2, a3 = _split3(a)
        acs = _dot(tri, a1) + _dot(tri, a2) + _dot(tri, a3)
        acs_t = acs.T
        dt_t = dt.T
        e_acs = jnp.exp(acs)
        last_t = acs_t[:, CHUNK - 1:CHUNK]
        w_t = dt_t * jnp.exp(last_t - acs_t)
        dec_t = jnp.exp(last_t)
        causal = col <= row
        left = col < SSM_P

        for g in range(SSM_GROUPS):
            bm = xc[:, D_SSM + g * SSM_N:D_SSM + (g + 1) * SSM_N]
            cm = xc[:, D_SSM + SSM_GROUPS * SSM_N + g * SSM_N:D_SSM + SSM_GROUPS * SSM_N + (g + 1) * SSM_N]
            bm16 = bm.astype(BF16)
            cm16 = cm.astype(BF16)
            cb = _dot_nt(cm16, bm16)
            bm_t = bm.T
            for pp in range(N_PAIRS // SSM_GROUPS):
                pair = g * (N_PAIRS // SSM_GROUPS) + pp
                xs_pair = xs[:, pair * LANES:(pair + 1) * LANES]
                prev = state_ref[b, pair]
                rhs_x = [jnp.where(left, xs_pair, 0.0).astype(BF16), jnp.where(left, 0.0, xs_pair).astype(BF16)]
                rhs_s = [jnp.where(left, prev, 0.0).astype(BF16), jnp.where(left, 0.0, prev).astype(BF16)]
                lhs_m, lhs_c, lhs_b, decs = [], [], [], []
                for k in range(2):
                    hd = 2 * pair + k
                    seg = acs[:, hd:hd + 1] - acs_t[hd:hd + 1, :]
                    lm = jnp.exp(jnp.where(causal, seg, NEG))
                    lhs_m.append((cb * lm * dt_t[hd:hd + 1, :]).astype(BF16))
                    lhs_c.append((cm * e_acs[:, hd:hd + 1]).astype(BF16))
                    lhs_b.append((bm_t * w_t[hd:hd + 1, :]).astype(BF16))
                    decs.append(jnp.broadcast_to(dec_t[hd:hd + 1, :], (SSM_N, LANES)))
                y_pair = _dot(jnp.concatenate(lhs_m + lhs_c, axis=1), jnp.concatenate(rhs_x + rhs_s, axis=0))
                new = _dot(jnp.concatenate(lhs_b, axis=1), jnp.concatenate(rhs_x, axis=0))
                state_ref[b, pair] = jnp.where(left, decs[0], decs[1]) * prev + new
                y_ref[b, :, pair * LANES:(pair + 1) * LANES] = y_pair

        y = (y_ref[b] + dskip_ref[...] * xs) * _silu(z_ref[b])
        gsz = D_SSM // SSM_GROUPS
        for g in range(SSM_GROUPS):
            yg = y[:, g * gsz:(g + 1) * gsz]
            ms = jnp.mean(yg * yg, axis=-1, keepdims=True)
            y_ref[b, :, g * gsz:(g + 1) * gsz] = yg * lax.rsqrt(ms + EPS) * ng_ref[:, g * gsz:(g + 1) * gsz]

    @pl.when(c < n_chunks)
    def _():
        for b in range(bsz):
            chunk(b)


def _ssd_call(z, xbc, dt, conv_w, conv_b, dt_bias, a_log, d_skip, norm_gain):
    bsz, lp, _ = z.shape
    n_blocks = lp // CHUNK
    pre_blk = (lp - ROW_TILE) // CHUNK
    n_chunks = pre_blk + 1

    def blk(_, c):
        return (0, jnp.where(c == 0, pre_blk, jnp.where(c < n_chunks, c - 1, c)), 0)

    row = lambda w: pl.BlockSpec((bsz, CHUNK, w), blk)
    return pl.pallas_call(
        functools.partial(_ssd_kernel, n_chunks),
        grid=(1, n_blocks),
        in_specs=[row(D_SSM), row(D_XBC), row(LANES),
                  _const_spec((SSM_CONV, D_XBC)), _const_spec((1, D_XBC)), _const_spec((1, LANES)),
                  _const_spec((1, LANES)), _const_spec((1, D_SSM)), _const_spec((1, D_SSM))],
        out_specs=row(D_SSM),
        out_shape=jax.ShapeDtypeStruct((bsz, lp, D_SSM), F32),
        scratch_shapes=[pltpu.VMEM((bsz, SUBLANES + CHUNK, D_XBC), F32),
                        pltpu.VMEM((bsz, N_PAIRS, SSM_N, LANES), F32)],
        compiler_params=_params(),
        name="ssd",
    )(z, xbc, dt, conv_w, conv_b, dt_bias, a_log, d_skip, norm_gain)


def _shifted(u, tail, d):
    r = pltpu.roll(u, d, 0)
    row = lax.broadcasted_iota(jnp.int32, (SUBLANES, u.shape[1]), 0)
    head = jnp.where(row < d, pltpu.roll(tail, d, 0), r[0:SUBLANES])
    return jnp.concatenate([head, r[SUBLANES:]], axis=0)


def _tail_kernel(x_ref, attn_ref, ssm_ref, xm_ref, attnm_ref, ssmm_ref, ga_ref, wout_ref, gmix_ref,
                 gpre_ref, wu_ref, cw_ref, wd_ref, gpost_ref, out_ref, tail_ref):
    i = pl.program_id(1)
    g_pre = gpre_ref[...]
    cols = lambda c, half: slice(half * D_FF + c * FF_CHUNK, half * D_FF + (c + 1) * FF_CHUNK)

    def mixed(h, attn, ssm):
        an = _rms(attn, ga_ref[...]).astype(BF16)
        mix = _dot(an, wout_ref[0:D_ATTN, :]) + _dot(ssm.astype(BF16), wout_ref[D_ATTN:, :])
        return h + _rms(mix, gmix_ref[...])

    @pl.when(i == 0)
    def _():
        rows = slice(N_META - SUBLANES, N_META)
        hm = _rms(mixed(xm_ref[...], attnm_ref[0, rows, :], ssmm_ref[0, rows, :]), g_pre).astype(BF16)
        for c in range(2 * N_FF_CHUNKS):
            sl = slice(c * FF_CHUNK, (c + 1) * FF_CHUNK)
            tail_ref[:, sl] = _dot(hm, wu_ref[:, sl])

    half_rows = FFN_TILE // 2
    halves = [slice(k * half_rows, (k + 1) * half_rows) for k in range(2)]
    h1 = [mixed(x_ref[0, hs, :], attn_ref[0, hs, :], ssm_ref[0, hs, :]) for hs in halves]
    hn = [_rms(h, g_pre).astype(BF16) for h in h1]

    def conv(u, sl):
        tails = [tail_ref[:, sl], u[0][half_rows - SUBLANES:half_rows, :]]
        tail_ref[:, sl] = u[1][half_rows - SUBLANES:half_rows, :]
        return [cw_ref[FFN_CONV:FFN_CONV + 1, sl] + cw_ref[2:3, sl] * uk + cw_ref[1:2, sl] * _shifted(uk, tk, 1)
                + cw_ref[0:1, sl] * _shifted(uk, tk, 2) for uk, tk in zip(u, tails)]

    def up(c):
        return [[_dot(hn[k], wu_ref[:, cols(c, part)]) for k in range(2)] for part in range(2)]

    def gate(c, u):
        g, v = conv(u[0], cols(c, 0)), conv(u[1], cols(c, 1))
        return [(_silu(g[k]) * v[k]).astype(BF16) for k in range(2)]

    u = {0: up(0), 1: up(1)}
    act = gate(0, u.pop(0))
    down = [None, None]
    for c in range(N_FF_CHUNKS):
        if c + 2 < N_FF_CHUNKS:
            u[c + 2] = up(c + 2)
        for k in range(2):
            d = _dot(act[k], wd_ref[c * FF_CHUNK:(c + 1) * FF_CHUNK, :])
            down[k] = d if down[k] is None else down[k] + d
        if c + 1 < N_FF_CHUNKS:
            act = gate(c + 1, u.pop(c + 1))
    for k in range(2):
        out_ref[0, halves[k], :] = h1[k] + _rms(down[k], gpost_ref[...])


def _tail_call(x, attn, ssm, meta_rows, gain_attn, w_out, gain_mix, gain_pre, w_up, conv_wb, w_down, gain_post):
    bsz, seq, _ = x.shape
    row = pl.BlockSpec((1, FFN_TILE, D_MODEL), lambda b, i: (b, i, 0))
    prefix = pl.BlockSpec((1, BLK, D_MODEL), lambda b, i: (b, seq // BLK, 0))
    vec = _const_spec((1, D_MODEL))
    return pl.pallas_call(
        _tail_kernel,
        grid=(bsz, seq // FFN_TILE),
        in_specs=[row, row, row, _const_spec((SUBLANES, D_MODEL)), prefix, prefix,
                  vec, _const_spec((D_ATTN + D_SSM, D_MODEL)), vec, vec,
                  _const_spec((D_MODEL, 2 * D_FF)), _const_spec((SUBLANES, 2 * D_FF)),
                  _const_spec((D_FF, D_MODEL)), vec],
        out_specs=row,
        out_shape=jax.ShapeDtypeStruct((bsz, seq, D_MODEL), F32),
        scratch_shapes=[pltpu.VMEM((SUBLANES, 2 * D_FF), F32)],
        compiler_params=_params(),
        name="tail",
    )(x, attn, ssm, meta_rows, attn, ssm, gain_attn, w_out, gain_mix, gain_pre, w_up, conv_wb, w_down, gain_post)


def _swap_halves(t):
    half = t.shape[-1] // 2
    return jnp.concatenate([t[..., half:], t[..., :half]], axis=-1)


def _rope_table(seq, lp):
    inv = ROPE_THETA ** (-jnp.arange(0, QK_ROPE, 2, dtype=F32) / QK_ROPE)

    def cs(pos):
        ang = pos.astype(F32)[:, None] * inv[None, :]
        return jnp.cos(ang), jnp.sin(ang)

    c_off, s_off = cs(N_META + ROW_TILE * jnp.arange(seq // ROW_TILE))
    c_row, s_row = cs(jnp.arange(ROW_TILE))
    cos = (c_off[:, None] * c_row[None] - s_off[:, None] * s_row[None]).reshape(seq, -1)
    sin = (s_off[:, None] * c_row[None] + c_off[:, None] * s_row[None]).reshape(seq, -1)
    c_meta, s_meta = cs(jnp.arange(N_META))
    tab = lambda c, s: jnp.concatenate([c, c, -s, s], axis=1)
    return jnp.concatenate([tab(cos, sin), tab(c_meta, s_meta),
                            jnp.zeros((lp - seq - N_META, LANES), F32)], axis=0)


def _pad_lanes(v, width=LANES):
    return jnp.zeros((1, width), F32).at[0, :v.shape[0]].set(v)


def kernel(x, meta_tokens, norm_mix_pre, norm_mix_post, norm_ffn_pre, norm_ffn_post, w_in, q_a_norm, w_uq,
           kv_a_norm, w_ukv, attn_out_norm, ssm_conv_w, ssm_conv_b, ssm_dt_bias, ssm_A_log, ssm_D, ssm_norm,
           w_out, w_up, ffn_conv_w, ffn_conv_b, w_down):
    bsz, seq, _ = x.shape
    lp = seq + ROW_TILE
    l = 0
    meta_tile = jnp.zeros((ROW_TILE, D_MODEL), F32).at[0:N_META].set(meta_tokens.astype(F32))

    o_kv, o_pe, o_z = Q_RANK, Q_RANK + KV_RANK, Q_RANK + KV_RANK + QK_ROPE
    o_xbc, o_dt = o_z + D_SSM, o_z + D_SSM + D_XBC
    wi = w_in[l]
    w_pe = wi[:, o_pe:o_z]
    w_in_p = jnp.concatenate(
        [wi[:, :o_pe], w_pe, _swap_halves(w_pe), wi[:, o_z:o_dt],
         wi[:, o_dt:], jnp.zeros((D_MODEL, LANES - SSM_HEADS), F32)], axis=1).astype(BF16)

    wq = w_uq[l] * (SOFTMAX_SCALE * math.log2(math.e))
    wq_pe = wq[:, :, QK_NOPE:]
    w_q = jnp.concatenate(
        [wq[:, :, :QK_NOPE].reshape(Q_RANK, D_ATTN),
         jnp.concatenate([wq_pe, _swap_halves(wq_pe)], axis=2).reshape(Q_RANK, D_ATTN)], axis=1).astype(BF16)
    wkv = w_ukv[l]
    w_kv = jnp.concatenate([wkv[:, :, :QK_NOPE].reshape(KV_RANK, D_ATTN),
                            wkv[:, :, QK_NOPE:].reshape(KV_RANK, D_ATTN)], axis=1).astype(BF16)

    q, k, vt, z, xbc, dt = _front_call(
        x, meta_tile, norm_mix_pre[l][None], w_in_p, q_a_norm[l][None], w_q,
        kv_a_norm[l][None], w_kv, _rope_table(seq, lp))

    attn = _attn_call(q, k, vt)

    ssm = _ssd_call(z, xbc, dt, ssm_conv_w[l], ssm_conv_b[l][None], _pad_lanes(ssm_dt_bias[l]),
                    _pad_lanes(ssm_A_log[l]), jnp.repeat(ssm_D[l], SSM_P)[None], ssm_norm[l][None])

    cw = jnp.concatenate([ffn_conv_w[l], ffn_conv_b[l][None],
                          jnp.zeros((SUBLANES - FFN_CONV - 1, 2 * D_FF), F32)], axis=0)
    return _tail_call(x, attn, ssm, meta_tokens[N_META - SUBLANES:].astype(F32), attn_out_norm[l][None],
                      w_out[l].astype(BF16), norm_mix_post[l][None], norm_ffn_pre[l][None],
                      w_up[l].astype(BF16), cw, w_down[l].astype(BF16), norm_ffn_post[l][None])
```

```python
import functools
import math

import jax
import jax.numpy as jnp
import numpy as np
from jax import lax
from jax.experimental import pallas as pl
from jax.experimental.pallas import tpu as pltpu

F32 = jnp.float32
BF16 = jnp.bfloat16

D_MODEL = 1024
N_META = 16
MLA_HEADS = 8
QK_NOPE = 128
QK_ROPE = 64
V_DIM = 128
Q_RANK = 384
KV_RANK = 256
ROPE_THETA = 10000.0
SOFTMAX_SCALE = (QK_NOPE + QK_ROPE) ** -0.5
D_ATTN = MLA_HEADS * V_DIM
SSM_HEADS = 16
SSM_P = 64
SSM_GROUPS = 2
SSM_N = 128
SSM_CONV = 4
CHUNK = 128
D_SSM = SSM_HEADS * SSM_P
D_XBC = D_SSM + 2 * SSM_GROUPS * SSM_N
D_FF = 2816
FFN_CONV = 3
EPS = 1e-6

LANES = 128
SUBLANES = 8
ROW_TILE = 512
BLK = 128
QK_PAD = 256
FF_CHUNK = 256
FFN_TILE = 512
ATTN_UNROLL = 24
N_FF_CHUNKS = D_FF // FF_CHUNK
LAT_W = 768
IN_W = LAT_W + D_SSM + D_XBC
NEG = -1e30
VMEM_LIMIT = 56 * 1024 * 1024


def _dot(a, b):
    return jnp.dot(a, b, preferred_element_type=F32)


def _dot_nt(a, b):
    return lax.dot_general(a, b, (((1,), (1,)), ((), ())), preferred_element_type=F32)


def _rms(x, gain):
    ms = jnp.mean(x * x, axis=-1, keepdims=True)
    return x * lax.rsqrt(ms + EPS) * gain


def _silu(x):
    h = 0.5 * x
    return h + h * jnp.tanh(h)


def _const_spec(shape):
    nd = len(shape)
    return pl.BlockSpec(shape, lambda *_: (0,) * nd, pipeline_mode=pl.Buffered(1))


def _params():
    return pltpu.CompilerParams(dimension_semantics=("arbitrary", "arbitrary"),
                                vmem_limit_bytes=VMEM_LIMIT)


def _front_kernel(n_real, x_ref, meta_ref, g_ref, win_ref, qan_ref, wq_ref, kvan_ref, wkv_ref, tab_ref,
                  qt_ref, k_ref, vt_ref, z_ref, xbc_ref, dt_ref):
    i = pl.program_id(1)
    half_rows = ROW_TILE // 2
    halves = [slice(k * half_rows, (k + 1) * half_rows) for k in range(2)]
    o_z, o_xbc = LAT_W, LAT_W + D_SSM
    lane = lax.broadcasted_iota(jnp.int32, (half_rows, LANES), 1)

    def normed(hs):
        h = jnp.where(i == n_real, meta_ref[hs, :], x_ref[0, hs, :])
        return _rms(h, g_ref[...]).astype(BF16)

    def project(hs, hn):
        lat = _dot(hn, win_ref[:, 0:LAT_W])
        z_ref[0, hs, :] = _dot(hn, win_ref[:, o_z:o_z + D_SSM])
        xbc_ref[0, hs, :] = _dot(hn, win_ref[:, o_xbc:o_xbc + D_XBC])
        tail = lat[:, Q_RANK + KV_RANK:LAT_W]
        dt_ref[0, hs, :] = jnp.where(lane < SSM_HEADS, pltpu.roll(tail, QK_ROPE, 1), 0.0)
        return lat

    def latent_norms(lat):
        return (_rms(lat[:, 0:Q_RANK], qan_ref[...]).astype(BF16),
                _rms(lat[:, Q_RANK:Q_RANK + KV_RANK], kvan_ref[...]).astype(BF16))

    def store_heads(hs, lat, q, kv):
        tab = tab_ref[hs, :]

        def rope(pair):
            a = pair * tab
            return jnp.where(lane < QK_ROPE, a + pltpu.roll(a, QK_ROPE, 1), 0.0)

        kp = lat[:, Q_RANK + KV_RANK:LAT_W]
        half = QK_ROPE // 2
        kp = jnp.where(lane < QK_ROPE, kp,
                       jnp.where(lane < QK_ROPE + half, pltpu.roll(kp, half, 1), pltpu.roll(kp, QK_ROPE + half, 1)))
        k_pe = rope(kp).astype(BF16)
        for hd in range(MLA_HEADS):
            sl = slice(hd * LANES, (hd + 1) * LANES)
            up = slice(D_ATTN + hd * LANES, D_ATTN + (hd + 1) * LANES)
            qt_ref[0, hd, 0, 0:LANES, hs] = q[:, sl].T.astype(BF16)
            qt_ref[0, hd, 0, LANES:QK_PAD, hs] = rope(q[:, up]).T.astype(BF16)
            k_ref[0, hd, hs, 0:LANES] = kv[:, sl].astype(BF16)
            k_ref[0, hd, hs, LANES:QK_PAD] = k_pe
            vt_ref[0, hd, 0, :, hs] = kv[:, up].T.astype(BF16)

    hn = [normed(hs) for hs in halves]
    lat0 = project(halves[0], hn[0])
    qn0, kvn0 = latent_norms(lat0)
    lat1 = project(halves[1], hn[1])
    q0, kv0 = _dot(qn0, wq_ref[...]), _dot(kvn0, wkv_ref[...])
    qn1, kvn1 = latent_norms(lat1)
    store_heads(halves[0], lat0, q0, kv0)
    q1, kv1 = _dot(qn1, wq_ref[...]), _dot(kvn1, wkv_ref[...])
    store_heads(halves[1], lat1, q1, kv1)


def _front_call(x, meta_tile, gain, w_in, q_an, w_q, kv_an, w_kv, rope_tab):
    bsz, seq, _ = x.shape
    n_real = seq // ROW_TILE
    n_tiles = n_real + 1
    lp = n_tiles * ROW_TILE
    row = lambda w: pl.BlockSpec((1, ROW_TILE, w), lambda b, i: (b, i, 0))
    tab = pl.BlockSpec((ROW_TILE, LANES), lambda b, i: (i, 0))
    qk_spec = pl.BlockSpec((1, MLA_HEADS, ROW_TILE, QK_PAD), lambda b, i: (b, 0, i, 0))
    return pl.pallas_call(
        functools.partial(_front_kernel, n_real),
        grid=(bsz, n_tiles),
        in_specs=[
            pl.BlockSpec((1, ROW_TILE, D_MODEL), lambda b, i: (b, jnp.minimum(i, n_real - 1), 0)),
            _const_spec((ROW_TILE, D_MODEL)),
            _const_spec((1, D_MODEL)),
            _const_spec((D_MODEL, IN_W)),
            _const_spec((1, Q_RANK)),
            _const_spec((Q_RANK, 2 * D_ATTN)),
            _const_spec((1, KV_RANK)),
            _const_spec((KV_RANK, 2 * D_ATTN)),
            tab,
        ],
        out_specs=[
            pl.BlockSpec((1, MLA_HEADS, 1, QK_PAD, ROW_TILE), lambda b, i: (b, 0, i, 0, 0)),
            qk_spec,
            pl.BlockSpec((1, MLA_HEADS, 1, V_DIM, ROW_TILE), lambda b, i: (b, 0, i, 0, 0)),
            row(D_SSM), row(D_XBC), row(LANES),
        ],
        out_shape=[
            jax.ShapeDtypeStruct((bsz, MLA_HEADS, n_tiles, QK_PAD, ROW_TILE), BF16),
            jax.ShapeDtypeStruct((bsz, MLA_HEADS, lp, QK_PAD), BF16),
            jax.ShapeDtypeStruct((bsz, MLA_HEADS, n_tiles, V_DIM, ROW_TILE), BF16),
            jax.ShapeDtypeStruct((bsz, lp, D_SSM), F32),
            jax.ShapeDtypeStruct((bsz, lp, D_XBC), F32),
            jax.ShapeDtypeStruct((bsz, lp, LANES), F32),
        ],
        compiler_params=_params(),
        name="front",
    )(x, meta_tile, gain, w_in, q_an, w_q, kv_an, w_kv, rope_tab)


def _attn_kernel(n_real, qt_ref, k_ref, vt_ref, o_ref, s0_ref, s1_ref, sm0_ref, sm1_ref, c0_ref, c1_ref,
                 m_ref, l_ref, acc_ref):
    pre = n_real * ROW_TILE
    s_refs, sm_refs, c_refs = (s0_ref, s1_ref), (sm0_ref, sm1_ref), (c0_ref, c1_ref)
    k_meta = k_ref[0, 0, pre:pre + N_META, :]
    vt_meta = vt_ref[0, 0, n_real, :, 0:N_META]

    def softmax_cols(s):
        m = jnp.max(s, axis=0, keepdims=True)
        p = jnp.exp2(s - m)
        return m, jnp.sum(p, axis=0, keepdims=True), p

    s = _dot(k_meta, qt_ref[0, 0, n_real, :, 0:BLK])
    kk = lax.broadcasted_iota(jnp.int32, s.shape, 0)
    qq = lax.broadcasted_iota(jnp.int32, s.shape, 1)
    _, l, p = softmax_cols(jnp.where(kk <= qq, s, NEG))
    o_ref[0, pre:pre + BLK, :] = (_dot(vt_meta, p.astype(BF16)) / l).T
    o_ref[0, pre + BLK:, :] = jnp.zeros((ROW_TILE - BLK, V_DIM), F32)

    def run(n_steps, first, advance, diagonal):
        def produce(slot, qi, j):
            qt = qt_ref[0, 0, jnp.minimum(qi, n_real - 1)]
            ks = pl.multiple_of(jnp.minimum(j, n_real - 1) * ROW_TILE, ROW_TILE)
            s = _dot(k_ref[0, 0, pl.ds(ks, ROW_TILE), :], qt)
            c = None
            if diagonal:
                kd = lax.broadcasted_iota(jnp.int32, s.shape, 0)
                qd = lax.broadcasted_iota(jnp.int32, s.shape, 1)
                s = jnp.where(kd <= qd, s, NEG)
                s_meta = _dot(k_meta, qt)
                sm_refs[slot][...] = s_meta
                c = jnp.max(s_meta, axis=0, keepdims=True)
            s_refs[slot][...] = s
            c_tile = jnp.max(s, axis=0, keepdims=True)
            c_refs[slot][...] = c_tile if c is None else jnp.maximum(c, c_tile)

        def step(slot, qi, j):
            nqi, nj = advance(qi, j)
            produce(1 - slot, nqi, nj)
            m_prev = m_ref[qi]
            m_new = jnp.maximum(m_prev, c_refs[slot][...])
            alpha = jnp.exp2(m_prev - m_new)
            p = jnp.exp2(s_refs[slot][...] - m_new)
            l = alpha * l_ref[qi] + jnp.sum(p, axis=0, keepdims=True)
            acc = alpha * acc_ref[qi] + _dot(vt_ref[0, 0, j], p.astype(BF16))
            if diagonal:
                p_meta = jnp.exp2(sm_refs[slot][...] - m_new)
                l = l + jnp.sum(p_meta, axis=0, keepdims=True)
                acc = acc + _dot(vt_meta, p_meta.astype(BF16))
                qs = pl.multiple_of(qi * ROW_TILE, ROW_TILE)
                o_ref[0, pl.ds(qs, ROW_TILE), :] = (acc / l).T
            else:
                m_ref[qi] = m_new
                l_ref[qi] = l
                acc_ref[qi] = acc
            return nqi, nj

        def body(_, carry):
            for u in range(ATTN_UNROLL):
                carry = step(u % 2, *carry)
            return carry

        if n_steps > 0:
            first = (jnp.int32(first[0]), jnp.int32(first[1]))
            produce(0, *first)
            carry = lax.fori_loop(0, n_steps // ATTN_UNROLL, body, first)
            for u in range(n_steps % ATTN_UNROLL):
                carry = step(u % 2, *carry)

    m_ref[...] = jnp.full(m_ref.shape, NEG, F32)
    l_ref[...] = jnp.zeros(l_ref.shape, F32)
    acc_ref[...] = jnp.zeros(acc_ref.shape, F32)

    def below(qi, j):
        wrap = j + 1 == qi
        return jnp.where(wrap, qi + 1, qi), jnp.where(wrap, 0, j + 1)

    run(n_real * (n_real - 1) // 2, (1, 0), below, False)
    run(n_real, (0, 0), lambda qi, j: (qi + 1, j + 1), True)


def _attn_call(qt, k, vt):
    bsz, _, lp, _ = k.shape
    n_real = lp // ROW_TILE - 1
    tiles = lambda rows: pl.BlockSpec((1, 1, n_real + 1, rows, ROW_TILE), lambda b, h: (b, h, 0, 0, 0))
    return pl.pallas_call(
        functools.partial(_attn_kernel, n_real),
        grid=(bsz, MLA_HEADS),
        in_specs=[tiles(QK_PAD), pl.BlockSpec((1, 1, lp, QK_PAD), lambda b, h: (b, h, 0, 0)), tiles(V_DIM)],
        out_specs=pl.BlockSpec((1, lp, V_DIM), lambda b, h: (b, 0, h)),
        out_shape=jax.ShapeDtypeStruct((bsz, lp, D_ATTN), F32),
        scratch_shapes=[
            pltpu.VMEM((ROW_TILE, ROW_TILE), F32), pltpu.VMEM((ROW_TILE, ROW_TILE), F32),
            pltpu.VMEM((N_META, ROW_TILE), F32), pltpu.VMEM((N_META, ROW_TILE), F32),
            pltpu.VMEM((1, ROW_TILE), F32), pltpu.VMEM((1, ROW_TILE), F32),
            pltpu.VMEM((n_real, 1, ROW_TILE), F32), pltpu.VMEM((n_real, 1, ROW_TILE), F32),
            pltpu.VMEM((n_real, V_DIM, ROW_TILE), F32),
        ],
        compiler_params=_params(),
        name="attn",
    )(qt, k, vt)


N_PAIRS = SSM_HEADS // 2


def _split3(a):
    a1 = a.astype(BF16)
    r1 = a - a1.astype(F32)
    a2 = r1.astype(BF16)
    a3 = (r1 - a2.astype(F32)).astype(BF16)
    return a1, a2, a3


def _ssd_kernel(n_chunks, z_ref, xbc_ref, dt_ref, cw_ref, cb_ref, dtb_ref, alog_ref, dskip_ref, ng_ref,
                y_ref, xbuf_ref, state_ref):
    c = pl.program_id(1)
    bsz = z_ref.shape[0]

    @pl.when(c == 0)
    def _():
        xbuf_ref[:, 0:SUBLANES, :] = jnp.zeros((bsz, SUBLANES, D_XBC), F32)
        state_ref[...] = jnp.zeros(state_ref.shape, F32)

    @pl.when(c >= n_chunks)
    def _():
        y_ref[...] = jnp.zeros(y_ref.shape, F32)

    def chunk(b):
        x = xbc_ref[b]
        xbuf_ref[b, SUBLANES:SUBLANES + CHUNK, :] = x
        conv = cb_ref[...] + cw_ref[SSM_CONV - 1:SSM_CONV, :] * x
        for d in range(1, SSM_CONV):
            conv = conv + (cw_ref[SSM_CONV - 1 - d:SSM_CONV - d, :]
                           * xbuf_ref[b, SUBLANES - d:SUBLANES - d + CHUNK, :])
        xbuf_ref[b, 0:SUBLANES, :] = jnp.where(c == 0, x[N_META - SUBLANES:N_META, :], x[CHUNK - SUBLANES:CHUNK, :])
        xc = _silu(conv)
        xs = xc[:, 0:D_SSM]

        row = lax.broadcasted_iota(jnp.int32, (CHUNK, LANES), 0)
        col = lax.broadcasted_iota(jnp.int32, (CHUNK, LANES), 1)
        dt = jax.nn.softplus(dt_ref[b] + dtb_ref[...])
        dt = jnp.where((c > 0) | (row < N_META), dt, 0.0)
        a = dt * (-jnp.exp(alog_ref[...]))
        tri = (col <= row).astype(BF16)
        a1, a2, a3 = _split3(a)
        acs = _dot(tri, a1) + _dot(tri, a2) + _dot(tri, a3)
        acs_t = acs.T
        dt_t = dt.T
        e_acs = jnp.exp(acs)
        last_t = acs_t[:, CHUNK - 1:CHUNK]
        w_t = dt_t * jnp.exp(last_t - acs_t)
        dec_t = jnp.exp(last_t)
        causal = col <= row
        left = col < SSM_P

        for g in range(SSM_GROUPS):
            bm = xc[:, D_SSM + g * SSM_N:D_SSM + (g + 1) * SSM_N]
            cm = xc[:, D_SSM + SSM_GROUPS * SSM_N + g * SSM_N:D_SSM + SSM_GROUPS * SSM_N + (g + 1) * SSM_N]
            bm16 = bm.astype(BF16)
            cm16 = cm.astype(BF16)
            cb = _dot_nt(cm16, bm16)
            bm_t = bm.T
            for pp in range(N_PAIRS // SSM_GROUPS):
                pair = g * (N_PAIRS // SSM_GROUPS) + pp
                xs_pair = xs[:, pair * LANES:(pair + 1) * LANES]
                prev = state_ref[b, pair]
                rhs_x = [jnp.where(left, xs_pair, 0.0).astype(BF16), jnp.where(left, 0.0, xs_pair).astype(BF16)]
                rhs_s = [jnp.where(left, prev, 0.0).astype(BF16), jnp.where(left, 0.0, prev).astype(BF16)]
                lhs_m, lhs_c, lhs_b, decs = [], [], [], []
                for k in range(2):
                    hd = 2 * pair + k
                    seg = acs[:, hd:hd + 1] - acs_t[hd:hd + 1, :]
                    lm = jnp.exp(jnp.where(causal, seg, NEG))
                    lhs_m.append((cb * lm * dt_t[hd:hd + 1, :]).astype(BF16))
                    lhs_c.append((cm * e_acs[:, hd:hd + 1]).astype(BF16))
                    lhs_b.append((bm_t * w_t[hd:hd + 1, :]).astype(BF16))
                    decs.append(jnp.broadcast_to(dec_t[hd:hd + 1, :], (SSM_N, LANES)))
                y_pair = _dot(jnp.concatenate(lhs_m + lhs_c, axis=1), jnp.concatenate(rhs_x + rhs_s, axis=0))
                new = _dot(jnp.concatenate(lhs_b, axis=1), jnp.concatenate(rhs_x, axis=0))
                state_ref[b, pair] = jnp.where(left, decs[0], decs[1]) * prev + new
                y_ref[b, :, pair * LANES:(pair + 1) * LANES] = y_pair

        y = (y_ref[b] + dskip_ref[...] * xs) * _silu(z_ref[b])
        gsz = D_SSM // SSM_GROUPS
        for g in range(SSM_GROUPS):
            yg = y[:, g * gsz:(g + 1) * gsz]
            ms = jnp.mean(yg * yg, axis=-1, keepdims=True)
            y_ref[b, :, g * gsz:(g + 1) * gsz] = yg * lax.rsqrt(ms + EPS) * ng_ref[:, g * gsz:(g + 1) * gsz]

    @pl.when(c < n_chunks)
    def _():
        for b in range(bsz):
            chunk(b)


def _ssd_call(z, xbc, dt, conv_w, conv_b, dt_bias, a_log, d_skip, norm_gain):
    bsz, lp, _ = z.shape
    n_blocks = lp // CHUNK
    pre_blk = (lp - ROW_TILE) // CHUNK
    n_chunks = pre_blk + 1

    def blk(_, c):
        return (0, jnp.where(c == 0, pre_blk, jnp.where(c < n_chunks, c - 1, c)), 0)

    row = lambda w: pl.BlockSpec((bsz, CHUNK, w), blk)
    return pl.pallas_call(
        functools.partial(_ssd_kernel, n_chunks),
        grid=(1, n_blocks),
        in_specs=[row(D_SSM), row(D_XBC), row(LANES),
                  _const_spec((SSM_CONV, D_XBC)), _const_spec((1, D_XBC)), _const_spec((1, LANES)),
                  _const_spec((1, LANES)), _const_spec((1, D_SSM)), _const_spec((1, D_SSM))],
        out_specs=row(D_SSM),
        out_shape=jax.ShapeDtypeStruct((bsz, lp, D_SSM), F32),
        scratch_shapes=[pltpu.VMEM((bsz, SUBLANES + CHUNK, D_XBC), F32),
                        pltpu.VMEM((bsz, N_PAIRS, SSM_N, LANES), F32)],
        compiler_params=_params(),
        name="ssd",
    )(z, xbc, dt, conv_w, conv_b, dt_bias, a_log, d_skip, norm_gain)


def _shifted(u, tail, d):
    r = pltpu.roll(u, d, 0)
    row = lax.broadcasted_iota(jnp.int32, (SUBLANES, u.shape[1]), 0)
    head = jnp.where(row < d, pltpu.roll(tail, d, 0), r[0:SUBLANES])
    return jnp.concatenate([head, r[SUBLANES:]], axis=0)


def _tail_kernel(x_ref, attn_ref, ssm_ref, xm_ref, attnm_ref, ssmm_ref, ga_ref, wout_ref, gmix_ref,
                 gpre_ref, wu_ref, cw_ref, wd_ref, gpost_ref, out_ref, tail_ref):
    i = pl.program_id(1)
    g_pre = gpre_ref[...]
    cols = lambda c, half: slice(half * D_FF + c * FF_CHUNK, half * D_FF + (c + 1) * FF_CHUNK)

    def mixed(h, attn, ssm):
        an = _rms(attn, ga_ref[...]).astype(BF16)
        mix = _dot(an, wout_ref[0:D_ATTN, :]) + _dot(ssm.astype(BF16), wout_ref[D_ATTN:, :])
        return h + _rms(mix, gmix_ref[...])

    @pl.when(i == 0)
    def _():
        rows = slice(N_META - SUBLANES, N_META)
        hm = _rms(mixed(xm_ref[...], attnm_ref[0, rows, :], ssmm_ref[0, rows, :]), g_pre).astype(BF16)
        for c in range(2 * N_FF_CHUNKS):
            sl = slice(c * FF_CHUNK, (c + 1) * FF_CHUNK)
            tail_ref[:, sl] = _dot(hm, wu_ref[:, sl])

    half_rows = FFN_TILE // 2
    halves = [slice(k * half_rows, (k + 1) * half_rows) for k in range(2)]
    h1 = [mixed(x_ref[0, hs, :], attn_ref[0, hs, :], ssm_ref[0, hs, :]) for hs in halves]
    hn = [_rms(h, g_pre).astype(BF16) for h in h1]

    def conv(u, sl):
        tails = [tail_ref[:, sl], u[0][half_rows - SUBLANES:half_rows, :]]
        tail_ref[:, sl] = u[1][half_rows - SUBLANES:half_rows, :]
        return [cw_ref[FFN_CONV:FFN_CONV + 1, sl] + cw_ref[2:3, sl] * uk + cw_ref[1:2, sl] * _shifted(uk, tk, 1)
                + cw_ref[0:1, sl] * _shifted(uk, tk, 2) for uk, tk in zip(u, tails)]

    def up(c):
        return [[_dot(hn[k], wu_ref[:, cols(c, part)]) for k in range(2)] for part in range(2)]

    def gate(c, u):
        g, v = conv(u[0], cols(c, 0)), conv(u[1], cols(c, 1))
        return [(_silu(g[k]) * v[k]).astype(BF16) for k in range(2)]

    u = {0: up(0), 1: up(1)}
    act = gate(0, u.pop(0))
    down = [None, None]
    for c in range(N_FF_CHUNKS):
        if c + 2 < N_FF_CHUNKS:
            u[c + 2] = up(c + 2)
        for k in range(2):
            d = _dot(act[k], wd_ref[c * FF_CHUNK:(c + 1) * FF_CHUNK, :])
            down[k] = d if down[k] is None else down[k] + d
        if c + 1 < N_FF_CHUNKS:
            act = gate(c + 1, u.pop(c + 1))
    for k in range(2):
        out_ref[0, halves[k], :] = h1[k] + _rms(down[k], gpost_ref[...])


def _tail_call(x, attn, ssm, meta_rows, gain_attn, w_out, gain_mix, gain_pre, w_up, conv_wb, w_down, gain_post):
    bsz, seq, _ = x.shape
    row = pl.BlockSpec((1, FFN_TILE, D_MODEL), lambda b, i: (b, i, 0))
    prefix = pl.BlockSpec((1, BLK, D_MODEL), lambda b, i: (b, seq // BLK, 0))
    vec = _const_spec((1, D_MODEL))
    return pl.pallas_call(
        _tail_kernel,
        grid=(bsz, seq // FFN_TILE),
        in_specs=[row, row, row, _const_spec((SUBLANES, D_MODEL)), prefix, prefix,
                  vec, _const_spec((D_ATTN + D_SSM, D_MODEL)), vec, vec,
                  _const_spec((D_MODEL, 2 * D_FF)), _const_spec((SUBLANES, 2 * D_FF)),
                  _const_spec((D_FF, D_MODEL)), vec],
        out_specs=row,
        out_shape=jax.ShapeDtypeStruct((bsz, seq, D_MODEL), F32),
        scratch_shapes=[pltpu.VMEM((SUBLANES, 2 * D_FF), F32)],
        compiler_params=_params(),
        name="tail",
    )(x, attn, ssm, meta_rows, attn, ssm, gain_attn, w_out, gain_mix, gain_pre, w_up, conv_wb, w_down, gain_post)


def _swap_halves(t):
    half = t.shape[-1] // 2
    return jnp.concatenate([t[..., half:], t[..., :half]], axis=-1)


def _rope_table(seq, lp):
    inv = ROPE_THETA ** (-jnp.arange(0, QK_ROPE, 2, dtype=F32) / QK_ROPE)

    def cs(pos):
        ang = pos.astype(F32)[:, None] * inv[None, :]
        return jnp.cos(ang), jnp.sin(ang)

    c_off, s_off = cs(N_META + ROW_TILE * jnp.arange(seq // ROW_TILE))
    c_row, s_row = cs(jnp.arange(ROW_TILE))
    cos = (c_off[:, None] * c_row[None] - s_off[:, None] * s_row[None]).reshape(seq, -1)
    sin = (s_off[:, None] * c_row[None] + c_off[:, None] * s_row[None]).reshape(seq, -1)
    c_meta, s_meta = cs(jnp.arange(N_META))
    tab = lambda c, s: jnp.concatenate([c, c, -s, s], axis=1)
    return jnp.concatenate([tab(cos, sin), tab(c_meta, s_meta),
                            jnp.zeros((lp - seq - N_META, LANES), F32)], axis=0)


def _pad_lanes(v, width=LANES):
    return jnp.zeros((1, width), F32).at[0, :v.shape[0]].set(v)


def kernel(x, meta_tokens, norm_mix_pre, norm_mix_post, norm_ffn_pre, norm_ffn_post, w_in, q_a_norm, w_uq,
           kv_a_norm, w_ukv, attn_out_norm, ssm_conv_w, ssm_conv_b, ssm_dt_bias, ssm_A_log, ssm_D, ssm_norm,
           w_out, w_up, ffn_conv_w, ffn_conv_b, w_down):
    bsz, seq, _ = x.shape
    lp = seq + ROW_TILE
    l = 0
    meta_tile = jnp.zeros((ROW_TILE, D_MODEL), F32).at[0:N_META].set(meta_tokens.astype(F32))

    o_z = Q_RANK + KV_RANK + QK_ROPE
    o_dt = o_z + D_SSM + D_XBC
    wi = w_in[l]
    w_in_p = jnp.concatenate(
        [wi[:, :o_z], wi[:, o_dt:], jnp.zeros((D_MODEL, LAT_W - o_z - SSM_HEADS), F32), wi[:, o_z:o_dt]],
        axis=1).astype(BF16)

    wq = w_uq[l] * (SOFTMAX_SCALE * math.log2(math.e))
    wq_pe = wq[:, :, QK_NOPE:]
    w_q = jnp.concatenate(
        [wq[:, :, :QK_NOPE].reshape(Q_RANK, D_ATTN),
         jnp.concatenate([wq_pe, _swap_halves(wq_pe)], axis=2).reshape(Q_RANK, D_ATTN)], axis=1).astype(BF16)
    wkv = w_ukv[l]
    w_kv = jnp.concatenate([wkv[:, :, :QK_NOPE].reshape(KV_RANK, D_ATTN),
                            wkv[:, :, QK_NOPE:].reshape(KV_RANK, D_ATTN)], axis=1).astype(BF16)

    q, k, vt, z, xbc, dt = _front_call(
        x, meta_tile, norm_mix_pre[l][None], w_in_p, q_a_norm[l][None], w_q,
        kv_a_norm[l][None], w_kv, _rope_table(seq, lp))

    attn = _attn_call(q, k, vt)

    ssm = _ssd_call(z, xbc, dt, ssm_conv_w[l], ssm_conv_b[l][None], _pad_lanes(ssm_dt_bias[l]),
                    _pad_lanes(ssm_A_log[l]), jnp.repeat(ssm_D[l], SSM_P)[None], ssm_norm[l][None])

    cw = jnp.concatenate([ffn_conv_w[l], ffn_conv_b[l][None],
                          jnp.zeros((SUBLANES - FFN_CONV - 1, 2 * D_FF), F32)], axis=0)
    return _tail_call(x, attn, ssm, meta_tokens[N_META - SUBLANES:].astype(F32), attn_out_norm[l][None],
                      w_out[l].astype(BF16), norm_mix_post[l][None], norm_ffn_pre[l][None],
                      w_up[l].astype(BF16), cw, w_down[l].astype(BF16), norm_ffn_post[l][None])
```

```python
import functools
import math

import jax
import jax.numpy as jnp
import numpy as np
from jax import lax
from jax.experimental import pallas as pl
from jax.experimental.pallas import tpu as pltpu

F32 = jnp.float32
BF16 = jnp.bfloat16

D_MODEL = 1024
N_META = 16
MLA_HEADS = 8
QK_NOPE = 128
QK_ROPE = 64
V_DIM = 128
Q_RANK = 384
KV_RANK = 256
ROPE_THETA = 10000.0
SOFTMAX_SCALE = (QK_NOPE + QK_ROPE) ** -0.5
D_ATTN = MLA_HEADS * V_DIM
SSM_HEADS = 16
SSM_P = 64
SSM_GROUPS = 2
SSM_N = 128
SSM_CONV = 4
CHUNK = 128
D_SSM = SSM_HEADS * SSM_P
D_XBC = D_SSM + 2 * SSM_GROUPS * SSM_N
D_FF = 2816
FFN_CONV = 3
EPS = 1e-6

LANES = 128
SUBLANES = 8
ROW_TILE = 512
BLK = 128
QK_PAD = 256
FF_CHUNK = 256
FFN_TILE = 512
ATTN_UNROLL = 40
N_FF_CHUNKS = D_FF // FF_CHUNK
LAT_W = 768
IN_W = LAT_W + D_SSM + D_XBC
NEG = -1e30
VMEM_LIMIT = 56 * 1024 * 1024


def _dot(a, b):
    return jnp.dot(a, b, preferred_element_type=F32)


def _dot_nt(a, b):
    return lax.dot_general(a, b, (((1,), (1,)), ((), ())), preferred_element_type=F32)


def _rms(x, gain):
    ms = jnp.mean(x * x, axis=-1, keepdims=True)
    return x * lax.rsqrt(ms + EPS) * gain


def _silu(x):
    h = 0.5 * x
    return h + h * jnp.tanh(h)


def _const_spec(shape):
    nd = len(shape)
    return pl.BlockSpec(shape, lambda *_: (0,) * nd, pipeline_mode=pl.Buffered(1))


def _params():
    return pltpu.CompilerParams(dimension_semantics=("arbitrary", "arbitrary"),
                                vmem_limit_bytes=VMEM_LIMIT)


def _front_kernel(n_real, x_ref, meta_ref, g_ref, win_ref, qan_ref, wq_ref, kvan_ref, wkv_ref, tab_ref,
                  qt_ref, k_ref, vt_ref, z_ref, xbc_ref, dt_ref):
    i = pl.program_id(1)
    half_rows = ROW_TILE // 2
    halves = [slice(k * half_rows, (k + 1) * half_rows) for k in range(2)]
    lane = lax.broadcasted_iota(jnp.int32, (half_rows, LANES), 1)
    low = lane < LANES // 2

    def normed(hs):
        h = jnp.where(i == n_real, meta_ref[hs, :], x_ref[0, hs, :])
        return _rms(h, g_ref[...]).astype(BF16)

    def project(hs, hn):
        lat = _dot(hn, win_ref[:, 0:LAT_W])
        rest = _dot(hn, win_ref[:, LAT_W:IN_W])
        n_groups = (IN_W - LAT_W) // LANES
        swapped = [pltpu.roll(lat[:, LAT_W - LANES:LAT_W], LANES // 2, 1)]
        swapped += [pltpu.roll(rest[:, g * LANES:(g + 1) * LANES], LANES // 2, 1) for g in range(n_groups)]
        group = lambda g: jnp.where(low, swapped[g], swapped[g + 1])
        for g in range(D_SSM // LANES):
            z_ref[0, hs, g * LANES:(g + 1) * LANES] = group(g)
        for g in range(D_XBC // LANES):
            xbc_ref[0, hs, g * LANES:(g + 1) * LANES] = group(D_SSM // LANES + g)
        dt_ref[0, hs, :] = jnp.where(lane < SSM_HEADS, swapped[n_groups], 0.0)
        return lat

    def latent_norms(lat):
        return (_rms(lat[:, 0:Q_RANK], qan_ref[...]).astype(BF16),
                _rms(lat[:, Q_RANK:Q_RANK + KV_RANK], kvan_ref[...]).astype(BF16))

    def store_heads(hs, lat, q, kv):
        tab = tab_ref[hs, :]

        def rope(pair):
            a = pair * tab
            return jnp.where(lane < QK_ROPE, a + pltpu.roll(a, QK_ROPE, 1), 0.0)

        kp = lat[:, Q_RANK + KV_RANK:LAT_W]
        half = QK_ROPE // 2
        kp = jnp.where(lane < QK_ROPE, kp,
                       jnp.where(lane < QK_ROPE + half, pltpu.roll(kp, half, 1), pltpu.roll(kp, QK_ROPE + half, 1)))
        k_pe = rope(kp).astype(BF16)
        for hd in range(MLA_HEADS):
            sl = slice(hd * LANES, (hd + 1) * LANES)
            up = slice(D_ATTN + hd * LANES, D_ATTN + (hd + 1) * LANES)
            qt_ref[0, hd, 0, 0:LANES, hs] = q[:, sl].T.astype(BF16)
            qt_ref[0, hd, 0, LANES:QK_PAD, hs] = rope(q[:, up]).T.astype(BF16)
            k_ref[0, hd, hs, 0:LANES] = kv[:, sl].astype(BF16)
            k_ref[0, hd, hs, LANES:QK_PAD] = k_pe
            vt_ref[0, hd, 0, :, hs] = kv[:, up].T.astype(BF16)

    hn = [normed(hs) for hs in halves]
    lat0 = project(halves[0], hn[0])
    qn0, kvn0 = latent_norms(lat0)
    lat1 = project(halves[1], hn[1])
    q0, kv0 = _dot(qn0, wq_ref[...]), _dot(kvn0, wkv_ref[...])
    qn1, kvn1 = latent_norms(lat1)
    store_heads(halves[0], lat0, q0, kv0)
    q1, kv1 = _dot(qn1, wq_ref[...]), _dot(kvn1, wkv_ref[...])
    store_heads(halves[1], lat1, q1, kv1)


def _front_call(x, meta_tile, gain, w_in, q_an, w_q, kv_an, w_kv, rope_tab):
    bsz, seq, _ = x.shape
    n_real = seq // ROW_TILE
    n_tiles = n_real + 1
    lp = n_tiles * ROW_TILE
    row = lambda w: pl.BlockSpec((1, ROW_TILE, w), lambda b, i: (b, i, 0))
    tab = pl.BlockSpec((ROW_TILE, LANES), lambda b, i: (i, 0))
    qk_spec = pl.BlockSpec((1, MLA_HEADS, ROW_TILE, QK_PAD), lambda b, i: (b, 0, i, 0))
    return pl.pallas_call(
        functools.partial(_front_kernel, n_real),
        grid=(bsz, n_tiles),
        in_specs=[
            pl.BlockSpec((1, ROW_TILE, D_MODEL), lambda b, i: (b, jnp.minimum(i, n_real - 1), 0)),
            _const_spec((ROW_TILE, D_MODEL)),
            _const_spec((1, D_MODEL)),
            _const_spec((D_MODEL, IN_W)),
            _const_spec((1, Q_RANK)),
            _const_spec((Q_RANK, 2 * D_ATTN)),
            _const_spec((1, KV_RANK)),
            _const_spec((KV_RANK, 2 * D_ATTN)),
            tab,
        ],
        out_specs=[
            pl.BlockSpec((1, MLA_HEADS, 1, QK_PAD, ROW_TILE), lambda b, i: (b, 0, i, 0, 0)),
            qk_spec,
            pl.BlockSpec((1, MLA_HEADS, 1, V_DIM, ROW_TILE), lambda b, i: (b, 0, i, 0, 0)),
            row(D_SSM), row(D_XBC), row(LANES),
        ],
        out_shape=[
            jax.ShapeDtypeStruct((bsz, MLA_HEADS, n_tiles, QK_PAD, ROW_TILE), BF16),
            jax.ShapeDtypeStruct((bsz, MLA_HEADS, lp, QK_PAD), BF16),
            jax.ShapeDtypeStruct((bsz, MLA_HEADS, n_tiles, V_DIM, ROW_TILE), BF16),
            jax.ShapeDtypeStruct((bsz, lp, D_SSM), F32),
            jax.ShapeDtypeStruct((bsz, lp, D_XBC), F32),
            jax.ShapeDtypeStruct((bsz, lp, LANES), F32),
        ],
        compiler_params=_params(),
        name="front",
    )(x, meta_tile, gain, w_in, q_an, w_q, kv_an, w_kv, rope_tab)


def _attn_kernel(n_real, qt_ref, k_ref, vt_ref, o_ref, s0_ref, s1_ref, sm0_ref, sm1_ref, c0_ref, c1_ref,
                 m_ref, l_ref, acc_ref):
    pre = n_real * ROW_TILE
    s_refs, sm_refs, c_refs = (s0_ref, s1_ref), (sm0_ref, sm1_ref), (c0_ref, c1_ref)
    k_meta = k_ref[0, 0, pre:pre + N_META, :]
    vt_meta = vt_ref[0, 0, n_real, :, 0:N_META]

    def softmax_cols(s):
        m = jnp.max(s, axis=0, keepdims=True)
        p = jnp.exp2(s - m)
        return m, jnp.sum(p, axis=0, keepdims=True), p

    s = _dot(k_meta, qt_ref[0, 0, n_real, :, 0:BLK])
    kk = lax.broadcasted_iota(jnp.int32, s.shape, 0)
    qq = lax.broadcasted_iota(jnp.int32, s.shape, 1)
    _, l, p = softmax_cols(jnp.where(kk <= qq, s, NEG))
    o_ref[0, pre:pre + BLK, :] = (_dot(vt_meta, p.astype(BF16)) / l).T
    o_ref[0, pre + BLK:, :] = jnp.zeros((ROW_TILE - BLK, V_DIM), F32)

    def run(n_steps, first, advance, diagonal):
        def produce(slot, qi, j):
            qt = qt_ref[0, 0, jnp.minimum(qi, n_real - 1)]
            ks = pl.multiple_of(jnp.minimum(j, n_real - 1) * ROW_TILE, ROW_TILE)
            s = _dot(k_ref[0, 0, pl.ds(ks, ROW_TILE), :], qt)
            c = None
            if diagonal:
                kd = lax.broadcasted_iota(jnp.int32, s.shape, 0)
                qd = lax.broadcasted_iota(jnp.int32, s.shape, 1)
                s = jnp.where(kd <= qd, s, NEG)
                s_meta = _dot(k_meta, qt)
                sm_refs[slot][...] = s_meta
                c = jnp.max(s_meta, axis=0, keepdims=True)
            s_refs[slot][...] = s
            c_tile = jnp.max(s, axis=0, keepdims=True)
            c_refs[slot][...] = c_tile if c is None else jnp.maximum(c, c_tile)

        def step(slot, qi, j):
            nqi, nj = advance(qi, j)
            produce(1 - slot, nqi, nj)
            m_prev = m_ref[qi]
            m_new = jnp.maximum(m_prev, c_refs[slot][...])
            alpha = jnp.exp2(m_prev - m_new)
            p = jnp.exp2(s_refs[slot][...] - m_new)
            l = alpha * l_ref[qi] + jnp.sum(p, axis=0, keepdims=True)
            acc = alpha * acc_ref[qi] + _dot(vt_ref[0, 0, j], p.astype(BF16))
            if diagonal:
                p_meta = jnp.exp2(sm_refs[slot][...] - m_new)
                l = l + jnp.sum(p_meta, axis=0, keepdims=True)
                acc = acc + _dot(vt_meta, p_meta.astype(BF16))
                qs = pl.multiple_of(qi * ROW_TILE, ROW_TILE)
                o_ref[0, pl.ds(qs, ROW_TILE), :] = (acc / l).T
            else:
                m_ref[qi] = m_new
                l_ref[qi] = l
                acc_ref[qi] = acc
            return nqi, nj

        def body(_, carry):
            for u in range(ATTN_UNROLL):
                carry = step(u % 2, *carry)
            return carry

        if n_steps > 0:
            first = (jnp.int32(first[0]), jnp.int32(first[1]))
            produce(0, *first)
            carry = lax.fori_loop(0, n_steps // ATTN_UNROLL, body, first)
            for u in range(n_steps % ATTN_UNROLL):
                carry = step(u % 2, *carry)

    m_ref[...] = jnp.full(m_ref.shape, NEG, F32)
    l_ref[...] = jnp.zeros(l_ref.shape, F32)
    acc_ref[...] = jnp.zeros(acc_ref.shape, F32)

    def below(qi, j):
        wrap = j + 1 == qi
        return jnp.where(wrap, qi + 1, qi), jnp.where(wrap, 0, j + 1)

    run(n_real * (n_real - 1) // 2, (1, 0), below, False)
    run(n_real, (0, 0), lambda qi, j: (qi + 1, j + 1), True)


def _attn_call(qt, k, vt):
    bsz, _, lp, _ = k.shape
    n_real = lp // ROW_TILE - 1
    tiles = lambda rows: pl.BlockSpec((1, 1, n_real + 1, rows, ROW_TILE), lambda b, h: (b, h, 0, 0, 0))
    return pl.pallas_call(
        functools.partial(_attn_kernel, n_real),
        grid=(bsz, MLA_HEADS),
        in_specs=[tiles(QK_PAD), pl.BlockSpec((1, 1, lp, QK_PAD), lambda b, h: (b, h, 0, 0)), tiles(V_DIM)],
        out_specs=pl.BlockSpec((1, lp, V_DIM), lambda b, h: (b, 0, h)),
        out_shape=jax.ShapeDtypeStruct((bsz, lp, D_ATTN), F32),
        scratch_shapes=[
            pltpu.VMEM((ROW_TILE, ROW_TILE), F32), pltpu.VMEM((ROW_TILE, ROW_TILE), F32),
            pltpu.VMEM((N_META, ROW_TILE), F32), pltpu.VMEM((N_META, ROW_TILE), F32),
            pltpu.VMEM((1, ROW_TILE), F32), pltpu.VMEM((1, ROW_TILE), F32),
            pltpu.VMEM((n_real, 1, ROW_TILE), F32), pltpu.VMEM((n_real, 1, ROW_TILE), F32),
            pltpu.VMEM((n_real, V_DIM, ROW_TILE), F32),
        ],
        compiler_params=_params(),
        name="attn",
    )(qt, k, vt)


N_PAIRS = SSM_HEADS // 2


def _split3(a):
    a1 = a.astype(BF16)
    r1 = a - a1.astype(F32)
    a2 = r1.astype(BF16)
    a3 = (r1 - a2.astype(F32)).astype(BF16)
    return a1, a2, a3


def _ssd_kernel(n_chunks, z_ref, xbc_ref, dt_ref, cw_ref, cb_ref, dtb_ref, alog_ref, dskip_ref, ng_ref,
                y_ref, xbuf_ref, state_ref):
    c = pl.program_id(1)
    bsz = z_ref.shape[0]

    @pl.when(c == 0)
    def _():
        xbuf_ref[:, 0:SUBLANES, :] = jnp.zeros((bsz, SUBLANES, D_XBC), F32)
        state_ref[...] = jnp.zeros(state_ref.shape, F32)

    @pl.when(c >= n_chunks)
    def _():
        y_ref[...] = jnp.zeros(y_ref.shape, F32)

    def chunk(b):
        x = xbc_ref[b]
        xbuf_ref[b, SUBLANES:SUBLANES + CHUNK, :] = x
        conv = cb_ref[...] + cw_ref[SSM_CONV - 1:SSM_CONV, :] * x
        for d in range(1, SSM_CONV):
            conv = conv + (cw_ref[SSM_CONV - 1 - d:SSM_CONV - d, :]
                           * xbuf_ref[b, SUBLANES - d:SUBLANES - d + CHUNK, :])
        xbuf_ref[b, 0:SUBLANES, :] = jnp.where(c == 0, x[N_META - SUBLANES:N_META, :], x[CHUNK - SUBLANES:CHUNK, :])
        xc = _silu(conv)
        xs = xc[:, 0:D_SSM]

        row = lax.broadcasted_iota(jnp.int32, (CHUNK, LANES), 0)
        col = lax.broadcasted_iota(jnp.int32, (CHUNK, LANES), 1)
        dt = jax.nn.softplus(dt_ref[b] + dtb_ref[...])
        dt = jnp.where((c > 0) | (row < N_META), dt, 0.0)
        a = dt * (-jnp.exp(alog_ref[...]))
        tri = (col <= row).astype(BF16)
        a1, a2, a3 = _split3(a)
        acs = _dot(tri, a1) + _dot(tri, a2) + _dot(tri, a3)
        acs_t = acs.T
        dt_t = dt.T
        e_acs = jnp.exp(acs)
        last_t = acs_t[:, CHUNK - 1:CHUNK]
        w_t = dt_t * jnp.exp(last_t - acs_t)
        dec_t = jnp.exp(last_t)
        causal = col <= row
        left = col < SSM_P

        for g in range(SSM_GROUPS):
            bm = xc[:, D_SSM + g * SSM_N:D_SSM + (g + 1) * SSM_N]
            cm = xc[:, D_SSM + SSM_GROUPS * SSM_N + g * SSM_N:D_SSM + SSM_GROUPS * SSM_N + (g + 1) * SSM_N]
            bm16 = bm.astype(BF16)
            cm16 = cm.astype(BF16)
            cb = _dot_nt(cm16, bm16)
            bm_t = bm.T
            for pp in range(N_PAIRS // SSM_GROUPS):
                pair = g * (N_PAIRS // SSM_GROUPS) + pp
                xs_pair = xs[:, pair * LANES:(pair + 1) * LANES]
                prev = state_ref[b, pair]
                rhs_x = [jnp.where(left, xs_pair, 0.0).astype(BF16), jnp.where(left, 0.0, xs_pair).astype(BF16)]
                rhs_s = [jnp.where(left, prev, 0.0).astype(BF16), jnp.where(left, 0.0, prev).astype(BF16)]
                lhs_m, lhs_c, lhs_b, decs = [], [], [], []
                for k in range(2):
                    hd = 2 * pair + k
                    seg = acs[:, hd:hd + 1] - acs_t[hd:hd + 1, :]
                    lm = jnp.exp(jnp.where(causal, seg, NEG))
                    lhs_m.append((cb * lm * dt_t[hd:hd + 1, :]).astype(BF16))
                    lhs_c.append((cm * e_acs[:, hd:hd + 1]).astype(BF16))
                    lhs_b.append((bm_t * w_t[hd:hd + 1, :]).astype(BF16))
                    decs.append(jnp.broadcast_to(dec_t[hd:hd + 1, :], (SSM_N, LANES)))
                y_pair = _dot(jnp.concatenate(lhs_m + lhs_c, axis=1), jnp.concatenate(rhs_x + rhs_s, axis=0))
                new = _dot(jnp.concatenate(lhs_b, axis=1), jnp.concatenate(rhs_x, axis=0))
                state_ref[b, pair] = jnp.where(left, decs[0], decs[1]) * prev + new
                y_ref[b, :, pair * LANES:(pair + 1) * LANES] = y_pair

        y = (y_ref[b] + dskip_ref[...] * xs) * _silu(z_ref[b])
        gsz = D_SSM // SSM_GROUPS
        for g in range(SSM_GROUPS):
            yg = y[:, g * gsz:(g + 1) * gsz]
            ms = jnp.mean(yg * yg, axis=-1, keepdims=True)
            y_ref[b, :, g * gsz:(g + 1) * gsz] = yg * lax.rsqrt(ms + EPS) * ng_ref[:, g * gsz:(g + 1) * gsz]

    @pl.when(c < n_chunks)
    def _():
        for b in range(bsz):
            chunk(b)


def _ssd_call(z, xbc, dt, conv_w, conv_b, dt_bias, a_log, d_skip, norm_gain):
    bsz, lp, _ = z.shape
    n_blocks = lp // CHUNK
    pre_blk = (lp - ROW_TILE) // CHUNK
    n_chunks = pre_blk + 1

    def blk(_, c):
        return (0, jnp.where(c == 0, pre_blk, jnp.where(c < n_chunks, c - 1, c)), 0)

    row = lambda w: pl.BlockSpec((bsz, CHUNK, w), blk)
    return pl.pallas_call(
        functools.partial(_ssd_kernel, n_chunks),
        grid=(1, n_blocks),
        in_specs=[row(D_SSM), row(D_XBC), row(LANES),
                  _const_spec((SSM_CONV, D_XBC)), _const_spec((1, D_XBC)), _const_spec((1, LANES)),
                  _const_spec((1, LANES)), _const_spec((1, D_SSM)), _const_spec((1, D_SSM))],
        out_specs=row(D_SSM),
        out_shape=jax.ShapeDtypeStruct((bsz, lp, D_SSM), F32),
        scratch_shapes=[pltpu.VMEM((bsz, SUBLANES + CHUNK, D_XBC), F32),
                        pltpu.VMEM((bsz, N_PAIRS, SSM_N, LANES), F32)],
        compiler_params=_params(),
        name="ssd",
    )(z, xbc, dt, conv_w, conv_b, dt_bias, a_log, d_skip, norm_gain)


def _shifted(u, tail, d):
    r = pltpu.roll(u, d, 0)
    row = lax.broadcasted_iota(jnp.int32, (SUBLANES, u.shape[1]), 0)
    head = jnp.where(row < d, pltpu.roll(tail, d, 0), r[0:SUBLANES])
    return jnp.concatenate([head, r[SUBLANES:]], axis=0)


def _tail_kernel(x_ref, attn_ref, ssm_ref, xm_ref, attnm_ref, ssmm_ref, ga_ref, wout_ref, gmix_ref,
                 gpre_ref, wu_ref, cw_ref, wd_ref, gpost_ref, out_ref, tail_ref):
    i = pl.program_id(1)
    g_pre = gpre_ref[...]
    cols = lambda c, half: slice(half * D_FF + c * FF_CHUNK, half * D_FF + (c + 1) * FF_CHUNK)

    def mixed(h, attn, ssm):
        an = _rms(attn, ga_ref[...]).astype(BF16)
        mix = _dot(an, wout_ref[0:D_ATTN, :]) + _dot(ssm.astype(BF16), wout_ref[D_ATTN:, :])
        return h + _rms(mix, gmix_ref[...])

    @pl.when(i == 0)
    def _():
        rows = slice(N_META - SUBLANES, N_META)
        hm = _rms(mixed(xm_ref[...], attnm_ref[0, rows, :], ssmm_ref[0, rows, :]), g_pre).astype(BF16)
        for c in range(2 * N_FF_CHUNKS):
            sl = slice(c * FF_CHUNK, (c + 1) * FF_CHUNK)
            tail_ref[:, sl] = _dot(hm, wu_ref[:, sl])

    half_rows = FFN_TILE // 2
    halves = [slice(k * half_rows, (k + 1) * half_rows) for k in range(2)]
    h1 = [mixed(x_ref[0, hs, :], attn_ref[0, hs, :], ssm_ref[0, hs, :]) for hs in halves]
    hn = [_rms(h, g_pre).astype(BF16) for h in h1]

    def conv(u, sl):
        tails = [tail_ref[:, sl], u[0][half_rows - SUBLANES:half_rows, :]]
        tail_ref[:, sl] = u[1][half_rows - SUBLANES:half_rows, :]
        return [cw_ref[FFN_CONV:FFN_CONV + 1, sl] + cw_ref[2:3, sl] * uk + cw_ref[1:2, sl] * _shifted(uk, tk, 1)
                + cw_ref[0:1, sl] * _shifted(uk, tk, 2) for uk, tk in zip(u, tails)]

    def up(c):
        return [[_dot(hn[k], wu_ref[:, cols(c, part)]) for k in range(2)] for part in range(2)]

    def gate(c, u):
        g, v = conv(u[0], cols(c, 0)), conv(u[1], cols(c, 1))
        return [(_silu(g[k]) * v[k]).astype(BF16) for k in range(2)]

    u = {0: up(0), 1: up(1)}
    act = gate(0, u.pop(0))
    down = [None, None]
    for c in range(N_FF_CHUNKS):
        if c + 2 < N_FF_CHUNKS:
            u[c + 2] = up(c + 2)
        for k in range(2):
            d = _dot(act[k], wd_ref[c * FF_CHUNK:(c + 1) * FF_CHUNK, :])
            down[k] = d if down[k] is None else down[k] + d
        if c + 1 < N_FF_CHUNKS:
            act = gate(c + 1, u.pop(c + 1))
    for k in range(2):
        out_ref[0, halves[k], :] = h1[k] + _rms(down[k], gpost_ref[...])


def _tail_call(x, attn, ssm, meta_rows, gain_attn, w_out, gain_mix, gain_pre, w_up, conv_wb, w_down, gain_post):
    bsz, seq, _ = x.shape
    row = pl.BlockSpec((1, FFN_TILE, D_MODEL), lambda b, i: (b, i, 0))
    prefix = pl.BlockSpec((1, BLK, D_MODEL), lambda b, i: (b, seq // BLK, 0))
    vec = _const_spec((1, D_MODEL))
    return pl.pallas_call(
        _tail_kernel,
        grid=(bsz, seq // FFN_TILE),
        in_specs=[row, row, row, _const_spec((SUBLANES, D_MODEL)), prefix, prefix,
                  vec, _const_spec((D_ATTN + D_SSM, D_MODEL)), vec, vec,
                  _const_spec((D_MODEL, 2 * D_FF)), _const_spec((SUBLANES, 2 * D_FF)),
                  _const_spec((D_FF, D_MODEL)), vec],
        out_specs=row,
        out_shape=jax.ShapeDtypeStruct((bsz, seq, D_MODEL), F32),
        scratch_shapes=[pltpu.VMEM((SUBLANES, 2 * D_FF), F32)],
        compiler_params=_params(),
        name="tail",
    )(x, attn, ssm, meta_rows, attn, ssm, gain_attn, w_out, gain_mix, gain_pre, w_up, conv_wb, w_down, gain_post)


def _swap_halves(t):
    half = t.shape[-1] // 2
    return jnp.concatenate([t[..., half:], t[..., :half]], axis=-1)


def _rope_table(seq, lp):
    inv = ROPE_THETA ** (-jnp.arange(0, QK_ROPE, 2, dtype=F32) / QK_ROPE)

    def cs(pos):
        ang = pos.astype(F32)[:, None] * inv[None, :]
        return jnp.cos(ang), jnp.sin(ang)

    c_off, s_off = cs(N_META + ROW_TILE * jnp.arange(seq // ROW_TILE))
    c_row, s_row = cs(jnp.arange(ROW_TILE))
    cos = (c_off[:, None] * c_row[None] - s_off[:, None] * s_row[None]).reshape(seq, -1)
    sin = (s_off[:, None] * c_row[None] + c_off[:, None] * s_row[None]).reshape(seq, -1)
    c_meta, s_meta = cs(jnp.arange(N_META))
    tab = lambda c, s: jnp.concatenate([c, c, -s, s], axis=1)
    return jnp.concatenate([tab(cos, sin), tab(c_meta, s_meta),
                            jnp.zeros((lp - seq - N_META, LANES), F32)], axis=0)


def _pad_lanes(v, width=LANES):
    return jnp.zeros((1, width), F32).at[0, :v.shape[0]].set(v)


def kernel(x, meta_tokens, norm_mix_pre, norm_mix_post, norm_ffn_pre, norm_ffn_post, w_in, q_a_norm, w_uq,
           kv_a_norm, w_ukv, attn_out_norm, ssm_conv_w, ssm_conv_b, ssm_dt_bias, ssm_A_log, ssm_D, ssm_norm,
           w_out, w_up, ffn_conv_w, ffn_conv_b, w_down):
    bsz, seq, _ = x.shape
    lp = seq + ROW_TILE
    l = 0
    meta_tile = jnp.zeros((ROW_TILE, D_MODEL), F32).at[0:N_META].set(meta_tokens.astype(F32))

    w_in_p = jnp.pad(w_in[l].astype(BF16), ((0, 0), (0, IN_W - w_in.shape[-1])))

    wq = w_uq[l] * (SOFTMAX_SCALE * math.log2(math.e))
    wq_pe = wq[:, :, QK_NOPE:]
    w_q = jnp.concatenate(
        [wq[:, :, :QK_NOPE].reshape(Q_RANK, D_ATTN),
         jnp.concatenate([wq_pe, _swap_halves(wq_pe)], axis=2).reshape(Q_RANK, D_ATTN)], axis=1).astype(BF16)
    wkv = w_ukv[l]
    w_kv = jnp.concatenate([wkv[:, :, :QK_NOPE].reshape(KV_RANK, D_ATTN),
                            wkv[:, :, QK_NOPE:].reshape(KV_RANK, D_ATTN)], axis=1).astype(BF16)

    q, k, vt, z, xbc, dt = _front_call(
        x, meta_tile, norm_mix_pre[l][None], w_in_p, q_a_norm[l][None], w_q,
        kv_a_norm[l][None], w_kv, _rope_table(seq, lp))

    attn = _attn_call(q, k, vt)

    ssm = _ssd_call(z, xbc, dt, ssm_conv_w[l], ssm_conv_b[l][None], _pad_lanes(ssm_dt_bias[l]),
                    _pad_lanes(ssm_A_log[l]), jnp.repeat(ssm_D[l], SSM_P)[None], ssm_norm[l][None])

    cw = jnp.concatenate([ffn_conv_w[l], ffn_conv_b[l][None],
                          jnp.zeros((SUBLANES - FFN_CONV - 1, 2 * D_FF), F32)], axis=0)
    return _tail_call(x, attn, ssm, meta_tokens[N_META - SUBLANES:].astype(F32), attn_out_norm[l][None],
                      w_out[l].astype(BF16), norm_mix_post[l][None], norm_ffn_pre[l][None],
                      w_up[l].astype(BF16), cw, w_down[l].astype(BF16), norm_ffn_post[l][None])
```

```python
import functools
import math

import jax
import jax.numpy as jnp
import numpy as np
from jax import lax
from jax.experimental import pallas as pl
from jax.experimental.pallas import tpu as pltpu

F32 = jnp.float32
BF16 = jnp.bfloat16

D_MODEL = 1024
N_META = 16
MLA_HEADS = 8
QK_NOPE = 128
QK_ROPE = 64
V_DIM = 128
Q_RANK = 384
KV_RANK = 256
ROPE_THETA = 10000.0
SOFTMAX_SCALE = (QK_NOPE + QK_ROPE) ** -0.5
D_ATTN = MLA_HEADS * V_DIM
SSM_HEADS = 16
SSM_P = 64
SSM_GROUPS = 2
SSM_N = 128
SSM_CONV = 4
CHUNK = 128
D_SSM = SSM_HEADS * SSM_P
D_XBC = D_SSM + 2 * SSM_GROUPS * SSM_N
D_FF = 2816
FFN_CONV = 3
EPS = 1e-6

LANES = 128
SUBLANES = 8
ROW_TILE = 512
BLK = 128
QK_PAD = 256
FF_CHUNK = 256
FFN_TILE = 512
ATTN_UNROLL = 40
N_FF_CHUNKS = D_FF // FF_CHUNK
LAT_W = 768
IN_W = LAT_W + D_SSM + D_XBC
NEG = -1e30
VMEM_LIMIT = 56 * 1024 * 1024


def _dot(a, b):
    return jnp.dot(a, b, preferred_element_type=F32)


def _dot_nt(a, b):
    return lax.dot_general(a, b, (((1,), (1,)), ((), ())), preferred_element_type=F32)


def _rms(x, gain):
    ms = jnp.mean(x * x, axis=-1, keepdims=True)
    return x * lax.rsqrt(ms + EPS) * gain


def _silu(x):
    h = 0.5 * x
    return h + h * jnp.tanh(h)


def _const_spec(shape):
    nd = len(shape)
    return pl.BlockSpec(shape, lambda *_: (0,) * nd, pipeline_mode=pl.Buffered(1))


def _params():
    return pltpu.CompilerParams(dimension_semantics=("arbitrary", "arbitrary"),
                                vmem_limit_bytes=VMEM_LIMIT)


def _front_kernel(n_real, x_ref, meta_ref, g_ref, win_ref, qan_ref, wq_ref, kvan_ref, wkv_ref, tab_ref,
                  qt_ref, k_ref, vt_ref, z_ref, xbc_ref, dt_ref):
    i = pl.program_id(1)
    half_rows = ROW_TILE // 2
    halves = [slice(k * half_rows, (k + 1) * half_rows) for k in range(2)]
    lane = lax.broadcasted_iota(jnp.int32, (half_rows, LANES), 1)
    low = lane < LANES // 2

    def normed(hs):
        h = jnp.where(i == n_real, meta_ref[hs, :], x_ref[0, hs, :])
        return _rms(h, g_ref[...]).astype(BF16)

    def project(hs, hn):
        lat = _dot(hn, win_ref[:, 0:LAT_W])
        rest = _dot(hn, win_ref[:, LAT_W:IN_W])
        n_groups = (IN_W - LAT_W) // LANES
        swapped = [pltpu.roll(lat[:, LAT_W - LANES:LAT_W], LANES // 2, 1)]
        swapped += [pltpu.roll(rest[:, g * LANES:(g + 1) * LANES], LANES // 2, 1) for g in range(n_groups)]
        group = lambda g: jnp.where(low, swapped[g], swapped[g + 1])
        for g in range(D_SSM // LANES):
            z_ref[0, hs, g * LANES:(g + 1) * LANES] = group(g)
        for g in range(D_XBC // LANES):
            xbc_ref[0, hs, g * LANES:(g + 1) * LANES] = group(D_SSM // LANES + g)
        dt_ref[0, hs, :] = jnp.where(lane < SSM_HEADS, swapped[n_groups], 0.0)
        return lat

    def latent_norms(lat):
        return (_rms(lat[:, 0:Q_RANK], qan_ref[...]).astype(BF16),
                _rms(lat[:, Q_RANK:Q_RANK + KV_RANK], kvan_ref[...]).astype(BF16))

    def store_heads(hs, lat, q, kv):
        tab = tab_ref[hs, :]

        def rope(pair):
            a = pair * tab
            return jnp.where(lane < QK_ROPE, a + pltpu.roll(a, QK_ROPE, 1), 0.0)

        kp = lat[:, Q_RANK + KV_RANK:LAT_W]
        half = QK_ROPE // 2
        kp = jnp.where(lane < QK_ROPE, kp,
                       jnp.where(lane < QK_ROPE + half, pltpu.roll(kp, half, 1), pltpu.roll(kp, QK_ROPE + half, 1)))
        k_pe = rope(kp).astype(BF16)
        for hd in range(MLA_HEADS):
            sl = slice(hd * LANES, (hd + 1) * LANES)
            up = slice(D_ATTN + hd * LANES, D_ATTN + (hd + 1) * LANES)
            qt_ref[0, hd, 0, 0:LANES, hs] = q[:, sl].T.astype(BF16)
            qt_ref[0, hd, 0, LANES:QK_PAD, hs] = rope(q[:, up]).T.astype(BF16)
            k_ref[0, hd, hs, 0:LANES] = kv[:, sl].astype(BF16)
            k_ref[0, hd, hs, LANES:QK_PAD] = k_pe
            vt_ref[0, hd, 0, :, hs] = kv[:, up].T.astype(BF16)

    hn = [normed(hs) for hs in halves]
    lat0 = project(halves[0], hn[0])
    qn0, kvn0 = latent_norms(lat0)
    lat1 = project(halves[1], hn[1])
    q0, kv0 = _dot(qn0, wq_ref[...]), _dot(kvn0, wkv_ref[...])
    qn1, kvn1 = latent_norms(lat1)
    store_heads(halves[0], lat0, q0, kv0)
    q1, kv1 = _dot(qn1, wq_ref[...]), _dot(kvn1, wkv_ref[...])
    store_heads(halves[1], lat1, q1, kv1)


def _front_call(x, meta_tile, gain, w_in, q_an, w_q, kv_an, w_kv, rope_tab):
    bsz, seq, _ = x.shape
    n_real = seq // ROW_TILE
    n_tiles = n_real + 1
    lp = n_tiles * ROW_TILE
    row = lambda w: pl.BlockSpec((1, ROW_TILE, w), lambda b, i: (b, i, 0))
    tab = pl.BlockSpec((ROW_TILE, LANES), lambda b, i: (i, 0))
    qk_spec = pl.BlockSpec((1, MLA_HEADS, ROW_TILE, QK_PAD), lambda b, i: (b, 0, i, 0))
    return pl.pallas_call(
        functools.partial(_front_kernel, n_real),
        grid=(bsz, n_tiles),
        in_specs=[
            pl.BlockSpec((1, ROW_TILE, D_MODEL), lambda b, i: (b, jnp.minimum(i, n_real - 1), 0)),
            _const_spec((ROW_TILE, D_MODEL)),
            _const_spec((1, D_MODEL)),
            _const_spec((D_MODEL, IN_W)),
            _const_spec((1, Q_RANK)),
            _const_spec((Q_RANK, 2 * D_ATTN)),
            _const_spec((1, KV_RANK)),
            _const_spec((KV_RANK, 2 * D_ATTN)),
            tab,
        ],
        out_specs=[
            pl.BlockSpec((1, MLA_HEADS, 1, QK_PAD, ROW_TILE), lambda b, i: (b, 0, i, 0, 0)),
            qk_spec,
            pl.BlockSpec((1, MLA_HEADS, 1, V_DIM, ROW_TILE), lambda b, i: (b, 0, i, 0, 0)),
            row(D_SSM), row(D_XBC), row(LANES),
        ],
        out_shape=[
            jax.ShapeDtypeStruct((bsz, MLA_HEADS, n_tiles, QK_PAD, ROW_TILE), BF16),
            jax.ShapeDtypeStruct((bsz, MLA_HEADS, lp, QK_PAD), BF16),
            jax.ShapeDtypeStruct((bsz, MLA_HEADS, n_tiles, V_DIM, ROW_TILE), BF16),
            jax.ShapeDtypeStruct((bsz, lp, D_SSM), F32),
            jax.ShapeDtypeStruct((bsz, lp, D_XBC), F32),
            jax.ShapeDtypeStruct((bsz, lp, LANES), F32),
        ],
        compiler_params=_params(),
        name="front",
    )(x, meta_tile, gain, w_in, q_an, w_q, kv_an, w_kv, rope_tab)


def _attn_kernel(n_real, qt_ref, k_ref, vt_ref, o_ref, s0_ref, s1_ref, sm0_ref, sm1_ref, c0_ref, c1_ref,
                 m_ref, l_ref, acc_ref):
    pre = n_real * ROW_TILE
    s_refs, sm_refs, c_refs = (s0_ref, s1_ref), (sm0_ref, sm1_ref), (c0_ref, c1_ref)
    k_meta = k_ref[0, 0, pre:pre + N_META, :]
    vt_meta = vt_ref[0, 0, n_real, :, 0:N_META]

    def softmax_cols(s):
        m = jnp.max(s, axis=0, keepdims=True)
        p = jnp.exp2(s - m)
        return m, jnp.sum(p, axis=0, keepdims=True), p

    s = _dot(k_meta, qt_ref[0, 0, n_real, :, 0:BLK])
    kk = lax.broadcasted_iota(jnp.int32, s.shape, 0)
    qq = lax.broadcasted_iota(jnp.int32, s.shape, 1)
    _, l, p = softmax_cols(jnp.where(kk <= qq, s, NEG))
    o_ref[0, pre:pre + BLK, :] = (_dot(vt_meta, p.astype(BF16)) / l).T
    o_ref[0, pre + BLK:, :] = jnp.zeros((ROW_TILE - BLK, V_DIM), F32)

    def run(n_steps, first, advance, diagonal):
        def produce(slot, qi, j):
            qt = qt_ref[0, 0, jnp.minimum(qi, n_real - 1)]
            ks = pl.multiple_of(jnp.minimum(j, n_real - 1) * ROW_TILE, ROW_TILE)
            s = _dot(k_ref[0, 0, pl.ds(ks, ROW_TILE), :], qt)
            c = None
            if diagonal:
                kd = lax.broadcasted_iota(jnp.int32, s.shape, 0)
                qd = lax.broadcasted_iota(jnp.int32, s.shape, 1)
                s = jnp.where(kd <= qd, s, NEG)
                s_meta = _dot(k_meta, qt)
                sm_refs[slot][...] = s_meta
                c = jnp.max(s_meta, axis=0, keepdims=True)
            s_refs[slot][...] = s
            c_tile = jnp.max(s, axis=0, keepdims=True)
            c_refs[slot][...] = c_tile if c is None else jnp.maximum(c, c_tile)

        def step(slot, qi, j):
            nqi, nj = advance(qi, j)
            produce(1 - slot, nqi, nj)
            m_prev = m_ref[qi]
            m_new = jnp.maximum(m_prev, c_refs[slot][...])
            alpha = jnp.exp2(m_prev - m_new)
            p = jnp.exp2(s_refs[slot][...] - m_new)
            l = alpha * l_ref[qi] + jnp.sum(p, axis=0, keepdims=True)
            acc = alpha * acc_ref[qi] + _dot(vt_ref[0, 0, j], p.astype(BF16))
            if diagonal:
                p_meta = jnp.exp2(sm_refs[slot][...] - m_new)
                l = l + jnp.sum(p_meta, axis=0, keepdims=True)
                acc = acc + _dot(vt_meta, p_meta.astype(BF16))
                qs = pl.multiple_of(qi * ROW_TILE, ROW_TILE)
                o_ref[0, pl.ds(qs, ROW_TILE), :] = (acc / l).T
            else:
                m_ref[qi] = m_new
                l_ref[qi] = l
                acc_ref[qi] = acc
            return nqi, nj

        def body(_, carry):
            for u in range(ATTN_UNROLL):
                carry = step(u % 2, *carry)
            return carry

        if n_steps > 0:
            first = (jnp.int32(first[0]), jnp.int32(first[1]))
            produce(0, *first)
            carry = lax.fori_loop(0, n_steps // ATTN_UNROLL, body, first)
            for u in range(n_steps % ATTN_UNROLL):
                carry = step(u % 2, *carry)

    m_ref[...] = jnp.full(m_ref.shape, NEG, F32)
    l_ref[...] = jnp.zeros(l_ref.shape, F32)
    acc_ref[...] = jnp.zeros(acc_ref.shape, F32)

    def below(qi, j):
        wrap = j + 1 == qi
        return jnp.where(wrap, qi + 1, qi), jnp.where(wrap, 0, j + 1)

    run(n_real * (n_real - 1) // 2, (1, 0), below, False)
    run(n_real, (0, 0), lambda qi, j: (qi + 1, j + 1), True)


def _attn_call(qt, k, vt):
    bsz, _, lp, _ = k.shape
    n_real = lp // ROW_TILE - 1
    tiles = lambda rows: pl.BlockSpec((1, 1, n_real + 1, rows, ROW_TILE), lambda b, h: (b, h, 0, 0, 0))
    return pl.pallas_call(
        functools.partial(_attn_kernel, n_real),
        grid=(bsz, MLA_HEADS),
        in_specs=[tiles(QK_PAD), pl.BlockSpec((1, 1, lp, QK_PAD), lambda b, h: (b, h, 0, 0)), tiles(V_DIM)],
        out_specs=pl.BlockSpec((1, lp, V_DIM), lambda b, h: (b, 0, h)),
        out_shape=jax.ShapeDtypeStruct((bsz, lp, D_ATTN), F32),
        scratch_shapes=[
            pltpu.VMEM((ROW_TILE, ROW_TILE), F32), pltpu.VMEM((ROW_TILE, ROW_TILE), F32),
            pltpu.VMEM((N_META, ROW_TILE), F32), pltpu.VMEM((N_META, ROW_TILE), F32),
            pltpu.VMEM((1, ROW_TILE), F32), pltpu.VMEM((1, ROW_TILE), F32),
            pltpu.VMEM((n_real, 1, ROW_TILE), F32), pltpu.VMEM((n_real, 1, ROW_TILE), F32),
            pltpu.VMEM((n_real, V_DIM, ROW_TILE), F32),
        ],
        compiler_params=_params(),
        name="attn",
    )(qt, k, vt)


N_PAIRS = SSM_HEADS // 2


def _split3(a):
    a1 = a.astype(BF16)
    r1 = a - a1.astype(F32)
    a2 = r1.astype(BF16)
    a3 = (r1 - a2.astype(F32)).astype(BF16)
    return a1, a2, a3


def _ssd_kernel(n_chunks, z_ref, xbc_ref, dt_ref, cw_ref, cb_ref, dtb_ref, alog_ref, dskip_ref, ng_ref,
                y_ref, xtail_ref, state_ref):
    c = pl.program_id(1)
    bsz = z_ref.shape[0]

    @pl.when(c == 0)
    def _():
        xtail_ref[...] = jnp.zeros(xtail_ref.shape, F32)
        state_ref[...] = jnp.zeros(state_ref.shape, F32)

    @pl.when(c >= n_chunks)
    def _():
        y_ref[...] = jnp.zeros(y_ref.shape, F32)

    def chunk(b):
        x = xbc_ref[b]
        tail = xtail_ref[b]
        first_row = lax.broadcasted_iota(jnp.int32, (SUBLANES, D_XBC), 0) == 0

        def delayed(t, t_tail):
            r = pltpu.roll(t, 1, 0)
            head = jnp.where(first_row, pltpu.roll(t_tail, 1, 0), r[0:SUBLANES])
            return jnp.concatenate([head, r[SUBLANES:]], axis=0)

        conv, conv_tail = cw_ref[0:1, :] * x, cw_ref[0:1, :] * tail
        for k in range(1, SSM_CONV):
            conv = cw_ref[k:k + 1, :] * x + delayed(conv, conv_tail)
            conv_tail = cw_ref[k:k + 1, :] * tail + pltpu.roll(conv_tail, 1, 0)
        xtail_ref[b] = jnp.where(c == 0, x[N_META - SUBLANES:N_META, :], x[CHUNK - SUBLANES:CHUNK, :])
        xc = _silu(conv + cb_ref[...])
        xs = xc[:, 0:D_SSM]

        row = lax.broadcasted_iota(jnp.int32, (CHUNK, LANES), 0)
        col = lax.broadcasted_iota(jnp.int32, (CHUNK, LANES), 1)
        dt = jax.nn.softplus(dt_ref[b] + dtb_ref[...])
        dt = jnp.where((c > 0) | (row < N_META), dt, 0.0)
        a = dt * (-jnp.exp(alog_ref[...]))
        tri = (col <= row).astype(BF16)
        a1, a2, a3 = _split3(a)
        acs = (_dot(tri, a1) + _dot(tri, a2) + _dot(tri, a3)) * math.log2(math.e)
        acs_t = acs.T
        dt_t = dt.T
        e_acs = jnp.exp2(acs)
        last_t = acs_t[:, CHUNK - 1:CHUNK]
        w_t = dt_t * jnp.exp2(last_t - acs_t)
        dec_t = jnp.exp2(last_t)
        causal = col <= row
        left = col < SSM_P

        for g in range(SSM_GROUPS):
            bm = xc[:, D_SSM + g * SSM_N:D_SSM + (g + 1) * SSM_N]
            cm = xc[:, D_SSM + SSM_GROUPS * SSM_N + g * SSM_N:D_SSM + SSM_GROUPS * SSM_N + (g + 1) * SSM_N]
            bm16 = bm.astype(BF16)
            cm16 = cm.astype(BF16)
            cb = _dot_nt(cm16, bm16)
            bm_t = bm.T
            for pp in range(N_PAIRS // SSM_GROUPS):
                pair = g * (N_PAIRS // SSM_GROUPS) + pp
                xs_pair = xs[:, pair * LANES:(pair + 1) * LANES]
                prev = state_ref[b, pair]
                rhs_x = [jnp.where(left, xs_pair, 0.0).astype(BF16), jnp.where(left, 0.0, xs_pair).astype(BF16)]
                rhs_s = [jnp.where(left, prev, 0.0).astype(BF16), jnp.where(left, 0.0, prev).astype(BF16)]
                lhs_m, lhs_c, lhs_b, decs = [], [], [], []
                for k in range(2):
                    hd = 2 * pair + k
                    seg = acs[:, hd:hd + 1] - acs_t[hd:hd + 1, :]
                    lm = jnp.exp2(jnp.where(causal, seg, NEG))
                    lhs_m.append((cb * lm * dt_t[hd:hd + 1, :]).astype(BF16))
                    lhs_c.append((cm * e_acs[:, hd:hd + 1]).astype(BF16))
                    lhs_b.append((bm_t * w_t[hd:hd + 1, :]).astype(BF16))
                    decs.append(jnp.broadcast_to(dec_t[hd:hd + 1, :], (SSM_N, LANES)))
                y_pair = _dot(jnp.concatenate(lhs_m + lhs_c, axis=1), jnp.concatenate(rhs_x + rhs_s, axis=0))
                new = _dot(jnp.concatenate(lhs_b, axis=1), jnp.concatenate(rhs_x, axis=0))
                state_ref[b, pair] = jnp.where(left, decs[0], decs[1]) * prev + new
                y_ref[b, :, pair * LANES:(pair + 1) * LANES] = y_pair

        y = (y_ref[b] + dskip_ref[...] * xs) * _silu(z_ref[b])
        gsz = D_SSM // SSM_GROUPS
        for g in range(SSM_GROUPS):
            yg = y[:, g * gsz:(g + 1) * gsz]
            ms = jnp.mean(yg * yg, axis=-1, keepdims=True)
            y_ref[b, :, g * gsz:(g + 1) * gsz] = yg * lax.rsqrt(ms + EPS) * ng_ref[:, g * gsz:(g + 1) * gsz]

    @pl.when(c < n_chunks)
    def _():
        for b in range(bsz):
            chunk(b)


def _ssd_call(z, xbc, dt, conv_w, conv_b, dt_bias, a_log, d_skip, norm_gain):
    bsz, lp, _ = z.shape
    n_blocks = lp // CHUNK
    pre_blk = (lp - ROW_TILE) // CHUNK
    n_chunks = pre_blk + 1

    def blk(_, c):
        return (0, jnp.where(c == 0, pre_blk, jnp.where(c < n_chunks, c - 1, c)), 0)

    row = lambda w: pl.BlockSpec((bsz, CHUNK, w), blk)
    return pl.pallas_call(
        functools.partial(_ssd_kernel, n_chunks),
        grid=(1, n_blocks),
        in_specs=[row(D_SSM), row(D_XBC), row(LANES),
                  _const_spec((SSM_CONV, D_XBC)), _const_spec((1, D_XBC)), _const_spec((1, LANES)),
                  _const_spec((1, LANES)), _const_spec((1, D_SSM)), _const_spec((1, D_SSM))],
        out_specs=row(D_SSM),
        out_shape=jax.ShapeDtypeStruct((bsz, lp, D_SSM), F32),
        scratch_shapes=[pltpu.VMEM((bsz, SUBLANES, D_XBC), F32),
                        pltpu.VMEM((bsz, N_PAIRS, SSM_N, LANES), F32)],
        compiler_params=_params(),
        name="ssd",
    )(z, xbc, dt, conv_w, conv_b, dt_bias, a_log, d_skip, norm_gain)


def _shifted(u, tail, d):
    r = pltpu.roll(u, d, 0)
    row = lax.broadcasted_iota(jnp.int32, (SUBLANES, u.shape[1]), 0)
    head = jnp.where(row < d, pltpu.roll(tail, d, 0), r[0:SUBLANES])
    return jnp.concatenate([head, r[SUBLANES:]], axis=0)


def _tail_kernel(x_ref, attn_ref, ssm_ref, xm_ref, attnm_ref, ssmm_ref, ga_ref, wout_ref, gmix_ref,
                 gpre_ref, wu_ref, cw_ref, wd_ref, gpost_ref, out_ref, tail_ref):
    i = pl.program_id(1)
    g_pre = gpre_ref[...]
    cols = lambda c, half: slice(half * D_FF + c * FF_CHUNK, half * D_FF + (c + 1) * FF_CHUNK)

    def mixed(h, attn, ssm):
        an = _rms(attn, ga_ref[...]).astype(BF16)
        mix = _dot(an, wout_ref[0:D_ATTN, :]) + _dot(ssm.astype(BF16), wout_ref[D_ATTN:, :])
        return h + _rms(mix, gmix_ref[...])

    @pl.when(i == 0)
    def _():
        rows = slice(N_META - SUBLANES, N_META)
        hm = _rms(mixed(xm_ref[...], attnm_ref[0, rows, :], ssmm_ref[0, rows, :]), g_pre).astype(BF16)
        for c in range(2 * N_FF_CHUNKS):
            sl = slice(c * FF_CHUNK, (c + 1) * FF_CHUNK)
            tail_ref[:, sl] = _dot(hm, wu_ref[:, sl])

    half_rows = FFN_TILE // 2
    halves = [slice(k * half_rows, (k + 1) * half_rows) for k in range(2)]
    h1 = [mixed(x_ref[0, hs, :], attn_ref[0, hs, :], ssm_ref[0, hs, :]) for hs in halves]
    hn = [_rms(h, g_pre).astype(BF16) for h in h1]

    def conv(u, sl):
        tails = [tail_ref[:, sl], u[0][half_rows - SUBLANES:half_rows, :]]
        tail_ref[:, sl] = u[1][half_rows - SUBLANES:half_rows, :]
        return [cw_ref[FFN_CONV:FFN_CONV + 1, sl] + cw_ref[2:3, sl] * uk + cw_ref[1:2, sl] * _shifted(uk, tk, 1)
                + cw_ref[0:1, sl] * _shifted(uk, tk, 2) for uk, tk in zip(u, tails)]

    def up(c):
        return [[_dot(hn[k], wu_ref[:, cols(c, part)]) for k in range(2)] for part in range(2)]

    def gate(c, u):
        g, v = conv(u[0], cols(c, 0)), conv(u[1], cols(c, 1))
        return [(_silu(g[k]) * v[k]).astype(BF16) for k in range(2)]

    u = {0: up(0), 1: up(1)}
    act = gate(0, u.pop(0))
    down = [None, None]
    for c in range(N_FF_CHUNKS):
        if c + 2 < N_FF_CHUNKS:
            u[c + 2] = up(c + 2)
        for k in range(2):
            d = _dot(act[k], wd_ref[c * FF_CHUNK:(c + 1) * FF_CHUNK, :])
            down[k] = d if down[k] is None else down[k] + d
        if c + 1 < N_FF_CHUNKS:
            act = gate(c + 1, u.pop(c + 1))
    for k in range(2):
        out_ref[0, halves[k], :] = h1[k] + _rms(down[k], gpost_ref[...])


def _tail_call(x, attn, ssm, meta_rows, gain_attn, w_out, gain_mix, gain_pre, w_up, conv_wb, w_down, gain_post):
    bsz, seq, _ = x.shape
    row = pl.BlockSpec((1, FFN_TILE, D_MODEL), lambda b, i: (b, i, 0))
    prefix = pl.BlockSpec((1, BLK, D_MODEL), lambda b, i: (b, seq // BLK, 0))
    vec = _const_spec((1, D_MODEL))
    return pl.pallas_call(
        _tail_kernel,
        grid=(bsz, seq // FFN_TILE),
        in_specs=[row, row, row, _const_spec((SUBLANES, D_MODEL)), prefix, prefix,
                  vec, _const_spec((D_ATTN + D_SSM, D_MODEL)), vec, vec,
                  _const_spec((D_MODEL, 2 * D_FF)), _const_spec((SUBLANES, 2 * D_FF)),
                  _const_spec((D_FF, D_MODEL)), vec],
        out_specs=row,
        out_shape=jax.ShapeDtypeStruct((bsz, seq, D_MODEL), F32),
        scratch_shapes=[pltpu.VMEM((SUBLANES, 2 * D_FF), F32)],
        compiler_params=_params(),
        name="tail",
    )(x, attn, ssm, meta_rows, attn, ssm, gain_attn, w_out, gain_mix, gain_pre, w_up, conv_wb, w_down, gain_post)


def _swap_halves(t):
    half = t.shape[-1] // 2
    return jnp.concatenate([t[..., half:], t[..., :half]], axis=-1)


def _rope_table(seq, lp):
    inv = ROPE_THETA ** (-jnp.arange(0, QK_ROPE, 2, dtype=F32) / QK_ROPE)

    def cs(pos):
        ang = pos.astype(F32)[:, None] * inv[None, :]
        return jnp.cos(ang), jnp.sin(ang)

    c_off, s_off = cs(N_META + ROW_TILE * jnp.arange(seq // ROW_TILE))
    c_row, s_row = cs(jnp.arange(ROW_TILE))
    cos = (c_off[:, None] * c_row[None] - s_off[:, None] * s_row[None]).reshape(seq, -1)
    sin = (s_off[:, None] * c_row[None] + c_off[:, None] * s_row[None]).reshape(seq, -1)
    c_meta, s_meta = cs(jnp.arange(N_META))
    tab = lambda c, s: jnp.concatenate([c, c, -s, s], axis=1)
    return jnp.concatenate([tab(cos, sin), tab(c_meta, s_meta),
                            jnp.zeros((lp - seq - N_META, LANES), F32)], axis=0)


def _pad_lanes(v, width=LANES):
    return jnp.zeros((1, width), F32).at[0, :v.shape[0]].set(v)


def kernel(x, meta_tokens, norm_mix_pre, norm_mix_post, norm_ffn_pre, norm_ffn_post, w_in, q_a_norm, w_uq,
           kv_a_norm, w_ukv, attn_out_norm, ssm_conv_w, ssm_conv_b, ssm_dt_bias, ssm_A_log, ssm_D, ssm_norm,
           w_out, w_up, ffn_conv_w, ffn_conv_b, w_down):
    bsz, seq, _ = x.shape
    lp = seq + ROW_TILE
    l = 0
    meta_tile = jnp.zeros((ROW_TILE, D_MODEL), F32).at[0:N_META].set(meta_tokens.astype(F32))

    w_in_p = jnp.pad(w_in[l].astype(BF16), ((0, 0), (0, IN_W - w_in.shape[-1])))

    wq = w_uq[l] * (SOFTMAX_SCALE * math.log2(math.e))
    wq_pe = wq[:, :, QK_NOPE:]
    w_q = jnp.concatenate(
        [wq[:, :, :QK_NOPE].reshape(Q_RANK, D_ATTN),
         jnp.concatenate([wq_pe, _swap_halves(wq_pe)], axis=2).reshape(Q_RANK, D_ATTN)], axis=1).astype(BF16)
    wkv = w_ukv[l]
    w_kv = jnp.concatenate([wkv[:, :, :QK_NOPE].reshape(KV_RANK, D_ATTN),
                            wkv[:, :, QK_NOPE:].reshape(KV_RANK, D_ATTN)], axis=1).astype(BF16)

    q, k, vt, z, xbc, dt = _front_call(
        x, meta_tile, norm_mix_pre[l][None], w_in_p, q_a_norm[l][None], w_q,
        kv_a_norm[l][None], w_kv, _rope_table(seq, lp))

    attn = _attn_call(q, k, vt)

    ssm = _ssd_call(z, xbc, dt, ssm_conv_w[l], ssm_conv_b[l][None], _pad_lanes(ssm_dt_bias[l]),
                    _pad_lanes(ssm_A_log[l]), jnp.repeat(ssm_D[l], SSM_P)[None], ssm_norm[l][None])

    cw = jnp.concatenate([ffn_conv_w[l], ffn_conv_b[l][None],
                          jnp.zeros((SUBLANES - FFN_CONV - 1, 2 * D_FF), F32)], axis=0)
    return _tail_call(x, attn, ssm, meta_tokens[N_META - SUBLANES:].astype(F32), attn_out_norm[l][None],
                      w_out[l].astype(BF16), norm_mix_post[l][None], norm_ffn_pre[l][None],
                      w_up[l].astype(BF16), cw, w_down[l].astype(BF16), norm_ffn_post[l][None])
```

```python
import functools
import math

import jax
import jax.numpy as jnp
from jax import lax
from jax.experimental import pallas as pl
from jax.experimental.pallas import tpu as pltpu

F32 = jnp.float32
BF16 = jnp.bfloat16

D_MODEL = 1024
N_META = 16
MLA_HEADS = 8
QK_NOPE = 128
QK_ROPE = 64
V_DIM = 128
Q_RANK = 384
KV_RANK = 256
ROPE_THETA = 10000.0
SOFTMAX_SCALE = (QK_NOPE + QK_ROPE) ** -0.5
D_ATTN = MLA_HEADS * V_DIM
SSM_HEADS = 16
SSM_P = 64
SSM_GROUPS = 2
SSM_N = 128
SSM_CONV = 4
CHUNK = 128
D_SSM = SSM_HEADS * SSM_P
D_XBC = D_SSM + 2 * SSM_GROUPS * SSM_N
D_FF = 2816
FFN_CONV = 3
EPS = 1e-6

LANES = 128
SUBLANES = 8
ROW_TILE = 512
BLK = 128
QK_PAD = 256
FF_CHUNK = 256
FFN_TILE = 512
ATTN_UNROLL = 40
N_FF_CHUNKS = D_FF // FF_CHUNK
LAT_W = 768
IN_W = LAT_W + D_SSM + D_XBC
NEG = -1e30
VMEM_LIMIT = 56 * 1024 * 1024


def _dot(a, b):
    return jnp.dot(a, b, preferred_element_type=F32)


def _dot_nt(a, b):
    return lax.dot_general(a, b, (((1,), (1,)), ((), ())), preferred_element_type=F32)


def _rms(x, gain):
    ms = jnp.mean(x * x, axis=-1, keepdims=True)
    return x * lax.rsqrt(ms + EPS) * gain


def _silu(x):
    h = 0.5 * x
    return h + h * jnp.tanh(h)


def _const_spec(shape):
    nd = len(shape)
    return pl.BlockSpec(shape, lambda *_: (0,) * nd, pipeline_mode=pl.Buffered(1))


def _params():
    return pltpu.CompilerParams(dimension_semantics=("arbitrary", "arbitrary"),
                                vmem_limit_bytes=VMEM_LIMIT)


def _front_kernel(n_real, x_ref, meta_ref, g_ref, win_ref, qan_ref, wq_ref, kvan_ref, wkv_ref, tab_ref,
                  qt_ref, k_ref, vt_ref, z_ref, xbc_ref, dt_ref):
    i = pl.program_id(1)
    half_rows = ROW_TILE // 2
    halves = [slice(k * half_rows, (k + 1) * half_rows) for k in range(2)]
    lane = lax.broadcasted_iota(jnp.int32, (half_rows, LANES), 1)
    low = lane < LANES // 2

    def normed(hs):
        h = jnp.where(i == n_real, meta_ref[hs, :], x_ref[0, hs, :])
        return _rms(h, g_ref[...]).astype(BF16)

    def project(hs, hn):
        lat = _dot(hn, win_ref[:, 0:LAT_W])
        rest = _dot(hn, win_ref[:, LAT_W:IN_W])
        n_groups = (IN_W - LAT_W) // LANES
        swapped = [pltpu.roll(lat[:, LAT_W - LANES:LAT_W], LANES // 2, 1)]
        swapped += [pltpu.roll(rest[:, g * LANES:(g + 1) * LANES], LANES // 2, 1) for g in range(n_groups)]
        group = lambda g: jnp.where(low, swapped[g], swapped[g + 1])
        for g in range(D_SSM // LANES):
            z_ref[0, hs, g * LANES:(g + 1) * LANES] = group(g)
        for g in range(D_XBC // LANES):
            xbc_ref[0, hs, g * LANES:(g + 1) * LANES] = group(D_SSM // LANES + g)
        dt_ref[0, hs, :] = jnp.where(lane < SSM_HEADS, swapped[n_groups], 0.0)
        return lat

    def latent_norms(lat):
        return (_rms(lat[:, 0:Q_RANK], qan_ref[...]).astype(BF16),
                _rms(lat[:, Q_RANK:Q_RANK + KV_RANK], kvan_ref[...]).astype(BF16))

    def store_heads(hs, lat, q, kv):
        tab = tab_ref[hs, :]

        def rope(pair):
            a = pair * tab
            return jnp.where(lane < QK_ROPE, a + pltpu.roll(a, QK_ROPE, 1), 0.0)

        kp = lat[:, Q_RANK + KV_RANK:LAT_W]
        half = QK_ROPE // 2
        kp = jnp.where(lane < QK_ROPE, kp,
                       jnp.where(lane < QK_ROPE + half, pltpu.roll(kp, half, 1), pltpu.roll(kp, QK_ROPE + half, 1)))
        k_pe = rope(kp).astype(BF16)
        for hd in range(MLA_HEADS):
            sl = slice(hd * LANES, (hd + 1) * LANES)
            up = slice(D_ATTN + hd * LANES, D_ATTN + (hd + 1) * LANES)
            qt_ref[0, hd, 0, 0:LANES, hs] = q[:, sl].T.astype(BF16)
            qt_ref[0, hd, 0, LANES:QK_PAD, hs] = rope(q[:, up]).T.astype(BF16)
            k_ref[0, hd, hs, 0:LANES] = kv[:, sl].astype(BF16)
            k_ref[0, hd, hs, LANES:QK_PAD] = k_pe
            vt_ref[0, hd, 0, :, hs] = kv[:, up].T.astype(BF16)

    hn = [normed(hs) for hs in halves]
    lat0 = project(halves[0], hn[0])
    qn0, kvn0 = latent_norms(lat0)
    lat1 = project(halves[1], hn[1])
    q0, kv0 = _dot(qn0, wq_ref[...]), _dot(kvn0, wkv_ref[...])
    qn1, kvn1 = latent_norms(lat1)
    store_heads(halves[0], lat0, q0, kv0)
    q1, kv1 = _dot(qn1, wq_ref[...]), _dot(kvn1, wkv_ref[...])
    store_heads(halves[1], lat1, q1, kv1)


def _front_call(x, meta_tile, gain, w_in, q_an, w_q, kv_an, w_kv, rope_tab):
    bsz, seq, _ = x.shape
    n_real = seq // ROW_TILE
    n_tiles = n_real + 1
    lp = n_tiles * ROW_TILE
    row = lambda w: pl.BlockSpec((1, ROW_TILE, w), lambda b, i: (b, i, 0))
    tab = pl.BlockSpec((ROW_TILE, LANES), lambda b, i: (i, 0))
    qk_spec = pl.BlockSpec((1, MLA_HEADS, ROW_TILE, QK_PAD), lambda b, i: (b, 0, i, 0))
    return pl.pallas_call(
        functools.partial(_front_kernel, n_real),
        grid=(bsz, n_tiles),
        in_specs=[
            pl.BlockSpec((1, ROW_TILE, D_MODEL), lambda b, i: (b, jnp.minimum(i, n_real - 1), 0)),
            _const_spec((ROW_TILE, D_MODEL)),
            _const_spec((1, D_MODEL)),
            _const_spec((D_MODEL, IN_W)),
            _const_spec((1, Q_RANK)),
            _const_spec((Q_RANK, 2 * D_ATTN)),
            _const_spec((1, KV_RANK)),
            _const_spec((KV_RANK, 2 * D_ATTN)),
            tab,
        ],
        out_specs=[
            pl.BlockSpec((1, MLA_HEADS, 1, QK_PAD, ROW_TILE), lambda b, i: (b, 0, i, 0, 0)),
            qk_spec,
            pl.BlockSpec((1, MLA_HEADS, 1, V_DIM, ROW_TILE), lambda b, i: (b, 0, i, 0, 0)),
            row(D_SSM), row(D_XBC), row(LANES),
        ],
        out_shape=[
            jax.ShapeDtypeStruct((bsz, MLA_HEADS, n_tiles, QK_PAD, ROW_TILE), BF16),
            jax.ShapeDtypeStruct((bsz, MLA_HEADS, lp, QK_PAD), BF16),
            jax.ShapeDtypeStruct((bsz, MLA_HEADS, n_tiles, V_DIM, ROW_TILE), BF16),
            jax.ShapeDtypeStruct((bsz, lp, D_SSM), F32),
            jax.ShapeDtypeStruct((bsz, lp, D_XBC), F32),
            jax.ShapeDtypeStruct((bsz, lp, LANES), F32),
        ],
        compiler_params=_params(),
        name="front",
    )(x, meta_tile, gain, w_in, q_an, w_q, kv_an, w_kv, rope_tab)


def _attn_kernel(n_real, qt_ref, k_ref, vt_ref, o_ref, s0_ref, s1_ref, sm0_ref, sm1_ref, c0_ref, c1_ref,
                 m_ref, l_ref, acc_ref):
    pre = n_real * ROW_TILE
    s_refs, sm_refs, c_refs = (s0_ref, s1_ref), (sm0_ref, sm1_ref), (c0_ref, c1_ref)
    k_meta = k_ref[0, 0, pre:pre + N_META, :]
    vt_meta = vt_ref[0, 0, n_real, :, 0:N_META]

    def softmax_cols(s):
        m = jnp.max(s, axis=0, keepdims=True)
        p = jnp.exp2(s - m)
        return m, jnp.sum(p, axis=0, keepdims=True), p

    s = _dot(k_meta, qt_ref[0, 0, n_real, :, 0:BLK])
    kk = lax.broadcasted_iota(jnp.int32, s.shape, 0)
    qq = lax.broadcasted_iota(jnp.int32, s.shape, 1)
    _, l, p = softmax_cols(jnp.where(kk <= qq, s, NEG))
    o_ref[0, pre:pre + BLK, :] = (_dot(vt_meta, p.astype(BF16)) / l).T
    o_ref[0, pre + BLK:, :] = jnp.zeros((ROW_TILE - BLK, V_DIM), F32)

    def run(n_steps, first, advance, diagonal):
        def produce(slot, qi, j):
            qt = qt_ref[0, 0, jnp.minimum(qi, n_real - 1)]
            ks = pl.multiple_of(jnp.minimum(j, n_real - 1) * ROW_TILE, ROW_TILE)
            s = _dot(k_ref[0, 0, pl.ds(ks, ROW_TILE), :], qt)
            c = None
            if diagonal:
                kd = lax.broadcasted_iota(jnp.int32, s.shape, 0)
                qd = lax.broadcasted_iota(jnp.int32, s.shape, 1)
                s = jnp.where(kd <= qd, s, NEG)
                s_meta = _dot(k_meta, qt)
                sm_refs[slot][...] = s_meta
                c = jnp.max(s_meta, axis=0, keepdims=True)
            s_refs[slot][...] = s
            c_tile = jnp.max(s, axis=0, keepdims=True)
            c_refs[slot][...] = c_tile if c is None else jnp.maximum(c, c_tile)

        def step(slot, qi, j):
            nqi, nj = advance(qi, j)
            produce(1 - slot, nqi, nj)
            m_prev = m_ref[qi]
            m_new = jnp.maximum(m_prev, c_refs[slot][...])
            alpha = jnp.exp2(m_prev - m_new)
            p = jnp.exp2(s_refs[slot][...] - m_new)
            l = alpha * l_ref[qi] + jnp.sum(p, axis=0, keepdims=True)
            acc = alpha * acc_ref[qi] + _dot(vt_ref[0, 0, j], p.astype(BF16))
            if diagonal:
                p_meta = jnp.exp2(sm_refs[slot][...] - m_new)
                l = l + jnp.sum(p_meta, axis=0, keepdims=True)
                acc = acc + _dot(vt_meta, p_meta.astype(BF16))
                qs = pl.multiple_of(qi * ROW_TILE, ROW_TILE)
                o_ref[0, pl.ds(qs, ROW_TILE), :] = (acc / l).T
            else:
                m_ref[qi] = m_new
                l_ref[qi] = l
                acc_ref[qi] = acc
            return nqi, nj

        def body(_, carry):
            for u in range(ATTN_UNROLL):
                carry = step(u % 2, *carry)
            return carry

        if n_steps > 0:
            first = (jnp.int32(first[0]), jnp.int32(first[1]))
            produce(0, *first)
            carry = lax.fori_loop(0, n_steps // ATTN_UNROLL, body, first)
            for u in range(n_steps % ATTN_UNROLL):
                carry = step(u % 2, *carry)

    m_ref[...] = jnp.full(m_ref.shape, NEG, F32)
    l_ref[...] = jnp.zeros(l_ref.shape, F32)
    acc_ref[...] = jnp.zeros(acc_ref.shape, F32)

    def below(qi, j):
        wrap = j + 1 == qi
        return jnp.where(wrap, qi + 1, qi), jnp.where(wrap, 0, j + 1)

    run(n_real * (n_real - 1) // 2, (1, 0), below, False)
    run(n_real, (0, 0), lambda qi, j: (qi + 1, j + 1), True)


def _attn_call(qt, k, vt):
    bsz, _, lp, _ = k.shape
    n_real = lp // ROW_TILE - 1
    tiles = lambda rows: pl.BlockSpec((1, 1, n_real + 1, rows, ROW_TILE), lambda b, h: (b, h, 0, 0, 0))
    return pl.pallas_call(
        functools.partial(_attn_kernel, n_real),
        grid=(bsz, MLA_HEADS),
        in_specs=[tiles(QK_PAD), pl.BlockSpec((1, 1, lp, QK_PAD), lambda b, h: (b, h, 0, 0)), tiles(V_DIM)],
        out_specs=pl.BlockSpec((1, lp, V_DIM), lambda b, h: (b, 0, h)),
        out_shape=jax.ShapeDtypeStruct((bsz, lp, D_ATTN), F32),
        scratch_shapes=[
            pltpu.VMEM((ROW_TILE, ROW_TILE), F32), pltpu.VMEM((ROW_TILE, ROW_TILE), F32),
            pltpu.VMEM((N_META, ROW_TILE), F32), pltpu.VMEM((N_META, ROW_TILE), F32),
            pltpu.VMEM((1, ROW_TILE), F32), pltpu.VMEM((1, ROW_TILE), F32),
            pltpu.VMEM((n_real, 1, ROW_TILE), F32), pltpu.VMEM((n_real, 1, ROW_TILE), F32),
            pltpu.VMEM((n_real, V_DIM, ROW_TILE), F32),
        ],
        compiler_params=_params(),
        name="attn",
    )(qt, k, vt)


N_PAIRS = SSM_HEADS // 2


def _split3(a):
    a1 = a.astype(BF16)
    r1 = a - a1.astype(F32)
    a2 = r1.astype(BF16)
    a3 = (r1 - a2.astype(F32)).astype(BF16)
    return a1, a2, a3


def _ssd_kernel(n_chunks, z_ref, xbc_ref, dt_ref, cw_ref, cb_ref, dtb_ref, alog_ref, dskip_ref, ng_ref,
                y_ref, xtail_ref, state_ref):
    c = pl.program_id(1)
    bsz = z_ref.shape[0]

    @pl.when(c == 0)
    def _():
        xtail_ref[...] = jnp.zeros(xtail_ref.shape, F32)
        state_ref[...] = jnp.zeros(state_ref.shape, F32)

    @pl.when(c >= n_chunks)
    def _():
        y_ref[...] = jnp.zeros(y_ref.shape, F32)

    def chunk(b):
        x = xbc_ref[b]
        tail = xtail_ref[b]
        first_row = lax.broadcasted_iota(jnp.int32, (SUBLANES, D_XBC), 0) == 0

        def delayed(t, t_tail):
            r = pltpu.roll(t, 1, 0)
            head = jnp.where(first_row, pltpu.roll(t_tail, 1, 0), r[0:SUBLANES])
            return jnp.concatenate([head, r[SUBLANES:]], axis=0)

        conv, conv_tail = cw_ref[0:1, :] * x, cw_ref[0:1, :] * tail
        for k in range(1, SSM_CONV):
            conv = cw_ref[k:k + 1, :] * x + delayed(conv, conv_tail)
            conv_tail = cw_ref[k:k + 1, :] * tail + pltpu.roll(conv_tail, 1, 0)
        xtail_ref[b] = jnp.where(c == 0, x[N_META - SUBLANES:N_META, :], x[CHUNK - SUBLANES:CHUNK, :])
        xc = _silu(conv + cb_ref[...])
        xs = xc[:, 0:D_SSM]

        row = lax.broadcasted_iota(jnp.int32, (CHUNK, LANES), 0)
        col = lax.broadcasted_iota(jnp.int32, (CHUNK, LANES), 1)
        dt = jax.nn.softplus(dt_ref[b] + dtb_ref[...])
        dt = jnp.where((c > 0) | (row < N_META), dt, 0.0)
        a = dt * (-jnp.exp(alog_ref[...]))
        tri = (col <= row).astype(BF16)
        a1, a2, a3 = _split3(a)
        acs = (_dot(tri, a1) + _dot(tri, a2) + _dot(tri, a3)) * math.log2(math.e)
        acs_t = acs.T
        dt_t = dt.T
        e_acs = jnp.exp2(acs)
        last_t = acs_t[:, CHUNK - 1:CHUNK]
        w_t = dt_t * jnp.exp2(last_t - acs_t)
        dec_t = jnp.exp2(last_t)
        causal = col <= row
        left = col < SSM_P

        for g in range(SSM_GROUPS):
            bm = xc[:, D_SSM + g * SSM_N:D_SSM + (g + 1) * SSM_N]
            cm = xc[:, D_SSM + SSM_GROUPS * SSM_N + g * SSM_N:D_SSM + SSM_GROUPS * SSM_N + (g + 1) * SSM_N]
            bm16 = bm.astype(BF16)
            cm16 = cm.astype(BF16)
            cb = _dot_nt(cm16, bm16)
            bm_t = bm.T
            for pp in range(N_PAIRS // SSM_GROUPS):
                pair = g * (N_PAIRS // SSM_GROUPS) + pp
                xs_pair = xs[:, pair * LANES:(pair + 1) * LANES]
                prev = state_ref[b, pair]
                rhs_x = [jnp.where(left, xs_pair, 0.0).astype(BF16), jnp.where(left, 0.0, xs_pair).astype(BF16)]
                rhs_s = [jnp.where(left, prev, 0.0).astype(BF16), jnp.where(left, 0.0, prev).astype(BF16)]
                lhs_m, lhs_c, lhs_b, decs = [], [], [], []
                for k in range(2):
                    hd = 2 * pair + k
                    seg = acs[:, hd:hd + 1] - acs_t[hd:hd + 1, :]
                    lm = jnp.exp2(jnp.where(causal, seg, NEG))
                    lhs_m.append((cb * lm * dt_t[hd:hd + 1, :]).astype(BF16))
                    lhs_c.append((cm * e_acs[:, hd:hd + 1]).astype(BF16))
                    lhs_b.append((bm_t * w_t[hd:hd + 1, :]).astype(BF16))
                    decs.append(jnp.broadcast_to(dec_t[hd:hd + 1, :], (SSM_N, LANES)))
                y_pair = _dot(jnp.concatenate(lhs_m + lhs_c, axis=1), jnp.concatenate(rhs_x + rhs_s, axis=0))
                new = _dot(jnp.concatenate(lhs_b, axis=1), jnp.concatenate(rhs_x, axis=0))
                state_ref[b, pair] = jnp.where(left, decs[0], decs[1]) * prev + new
                y_ref[b, :, pair * LANES:(pair + 1) * LANES] = y_pair

        y = (y_ref[b] + dskip_ref[...] * xs) * _silu(z_ref[b])
        gsz = D_SSM // SSM_GROUPS
        for g in range(SSM_GROUPS):
            yg = y[:, g * gsz:(g + 1) * gsz]
            ms = jnp.mean(yg * yg, axis=-1, keepdims=True)
            y_ref[b, :, g * gsz:(g + 1) * gsz] = yg * lax.rsqrt(ms + EPS) * ng_ref[:, g * gsz:(g + 1) * gsz]

    @pl.when(c < n_chunks)
    def _():
        for b in range(bsz):
            chunk(b)


def _ssd_call(z, xbc, dt, conv_w, conv_b, dt_bias, a_log, d_skip, norm_gain):
    bsz, lp, _ = z.shape
    n_blocks = lp // CHUNK
    pre_blk = (lp - ROW_TILE) // CHUNK
    n_chunks = pre_blk + 1

    def blk(_, c):
        return (0, jnp.where(c == 0, pre_blk, jnp.where(c < n_chunks, c - 1, c)), 0)

    row = lambda w: pl.BlockSpec((bsz, CHUNK, w), blk)
    return pl.pallas_call(
        functools.partial(_ssd_kernel, n_chunks),
        grid=(1, n_blocks),
        in_specs=[row(D_SSM), row(D_XBC), row(LANES),
                  _const_spec((SSM_CONV, D_XBC)), _const_spec((1, D_XBC)), _const_spec((1, LANES)),
                  _const_spec((1, LANES)), _const_spec((1, D_SSM)), _const_spec((1, D_SSM))],
        out_specs=row(D_SSM),
        out_shape=jax.ShapeDtypeStruct((bsz, lp, D_SSM), F32),
        scratch_shapes=[pltpu.VMEM((bsz, SUBLANES, D_XBC), F32),
                        pltpu.VMEM((bsz, N_PAIRS, SSM_N, LANES), F32)],
        compiler_params=_params(),
        name="ssd",
    )(z, xbc, dt, conv_w, conv_b, dt_bias, a_log, d_skip, norm_gain)


def _shifted(u, tail, d):
    r = pltpu.roll(u, d, 0)
    row = lax.broadcasted_iota(jnp.int32, (SUBLANES, u.shape[1]), 0)
    head = jnp.where(row < d, pltpu.roll(tail, d, 0), r[0:SUBLANES])
    return jnp.concatenate([head, r[SUBLANES:]], axis=0)


def _tail_kernel(x_ref, attn_ref, ssm_ref, xm_ref, attnm_ref, ssmm_ref, ga_ref, wout_ref, gmix_ref,
                 gpre_ref, wu_ref, cw_ref, wd_ref, gpost_ref, out_ref, tail_ref):
    i = pl.program_id(1)
    g_pre = gpre_ref[...]
    cols = lambda c, half: slice(half * D_FF + c * FF_CHUNK, half * D_FF + (c + 1) * FF_CHUNK)

    def mixed(h, attn, ssm):
        an = _rms(attn, ga_ref[...]).astype(BF16)
        mix = _dot(an, wout_ref[0:D_ATTN, :]) + _dot(ssm.astype(BF16), wout_ref[D_ATTN:, :])
        return h + _rms(mix, gmix_ref[...])

    @pl.when(i == 0)
    def _():
        rows = slice(N_META - SUBLANES, N_META)
        hm = _rms(mixed(xm_ref[...], attnm_ref[0, rows, :], ssmm_ref[0, rows, :]), g_pre).astype(BF16)
        for c in range(2 * N_FF_CHUNKS):
            sl = slice(c * FF_CHUNK, (c + 1) * FF_CHUNK)
            tail_ref[:, sl] = _dot(hm, wu_ref[:, sl])

    half_rows = FFN_TILE // 2
    halves = [slice(k * half_rows, (k + 1) * half_rows) for k in range(2)]
    h1 = [mixed(x_ref[0, hs, :], attn_ref[0, hs, :], ssm_ref[0, hs, :]) for hs in halves]
    hn = [_rms(h, g_pre).astype(BF16) for h in h1]

    def conv(u, sl):
        tails = [tail_ref[:, sl], u[0][half_rows - SUBLANES:half_rows, :]]
        tail_ref[:, sl] = u[1][half_rows - SUBLANES:half_rows, :]
        return [cw_ref[FFN_CONV:FFN_CONV + 1, sl] + cw_ref[2:3, sl] * uk + cw_ref[1:2, sl] * _shifted(uk, tk, 1)
                + cw_ref[0:1, sl] * _shifted(uk, tk, 2) for uk, tk in zip(u, tails)]

    def up(c):
        return [[_dot(hn[k], wu_ref[:, cols(c, part)]) for k in range(2)] for part in range(2)]

    def gate(c, u):
        g, v = conv(u[0], cols(c, 0)), conv(u[1], cols(c, 1))
        return [(_silu(g[k]) * v[k]).astype(BF16) for k in range(2)]

    u = {0: up(0), 1: up(1)}
    act = gate(0, u.pop(0))
    down = [None, None]
    for c in range(N_FF_CHUNKS):
        if c + 2 < N_FF_CHUNKS:
            u[c + 2] = up(c + 2)
        for k in range(2):
            d = _dot(act[k], wd_ref[c * FF_CHUNK:(c + 1) * FF_CHUNK, :])
            down[k] = d if down[k] is None else down[k] + d
        if c + 1 < N_FF_CHUNKS:
            act = gate(c + 1, u.pop(c + 1))
    for k in range(2):
        out_ref[0, halves[k], :] = h1[k] + _rms(down[k], gpost_ref[...])


def _tail_call(x, attn, ssm, meta_rows, gain_attn, w_out, gain_mix, gain_pre, w_up, conv_wb, w_down, gain_post):
    bsz, seq, _ = x.shape
    row = pl.BlockSpec((1, FFN_TILE, D_MODEL), lambda b, i: (b, i, 0))
    prefix = pl.BlockSpec((1, BLK, D_MODEL), lambda b, i: (b, seq // BLK, 0))
    vec = _const_spec((1, D_MODEL))
    return pl.pallas_call(
        _tail_kernel,
        grid=(bsz, seq // FFN_TILE),
        in_specs=[row, row, row, _const_spec((SUBLANES, D_MODEL)), prefix, prefix,
                  vec, _const_spec((D_ATTN + D_SSM, D_MODEL)), vec, vec,
                  _const_spec((D_MODEL, 2 * D_FF)), _const_spec((SUBLANES, 2 * D_FF)),
                  _const_spec((D_FF, D_MODEL)), vec],
        out_specs=row,
        out_shape=jax.ShapeDtypeStruct((bsz, seq, D_MODEL), F32),
        scratch_shapes=[pltpu.VMEM((SUBLANES, 2 * D_FF), F32)],
        compiler_params=_params(),
        name="tail",
    )(x, attn, ssm, meta_rows, attn, ssm, gain_attn, w_out, gain_mix, gain_pre, w_up, conv_wb, w_down, gain_post)


def _swap_halves(t):
    half = t.shape[-1] // 2
    return jnp.concatenate([t[..., half:], t[..., :half]], axis=-1)


def _rope_table(seq, lp):
    inv = ROPE_THETA ** (-jnp.arange(0, QK_ROPE, 2, dtype=F32) / QK_ROPE)

    def cs(pos):
        ang = pos.astype(F32)[:, None] * inv[None, :]
        return jnp.cos(ang), jnp.sin(ang)

    c_off, s_off = cs(N_META + ROW_TILE * jnp.arange(seq // ROW_TILE))
    c_row, s_row = cs(jnp.arange(ROW_TILE))
    cos = (c_off[:, None] * c_row[None] - s_off[:, None] * s_row[None]).reshape(seq, -1)
    sin = (s_off[:, None] * c_row[None] + c_off[:, None] * s_row[None]).reshape(seq, -1)
    c_meta, s_meta = cs(jnp.arange(N_META))
    tab = lambda c, s: jnp.concatenate([c, c, -s, s], axis=1)
    return jnp.concatenate([tab(cos, sin), tab(c_meta, s_meta),
                            jnp.zeros((lp - seq - N_META, LANES), F32)], axis=0)


def _pad_lanes(v, width=LANES):
    return jnp.zeros((1, width), F32).at[0, :v.shape[0]].set(v)


def kernel(x, meta_tokens, norm_mix_pre, norm_mix_post, norm_ffn_pre, norm_ffn_post, w_in, q_a_norm, w_uq,
           kv_a_norm, w_ukv, attn_out_norm, ssm_conv_w, ssm_conv_b, ssm_dt_bias, ssm_A_log, ssm_D, ssm_norm,
           w_out, w_up, ffn_conv_w, ffn_conv_b, w_down):
    bsz, seq, _ = x.shape
    lp = seq + ROW_TILE
    l = 0
    meta_tile = jnp.zeros((ROW_TILE, D_MODEL), F32).at[0:N_META].set(meta_tokens.astype(F32))

    w_in_p = jnp.pad(w_in[l].astype(BF16), ((0, 0), (0, IN_W - w_in.shape[-1])))

    wq = w_uq[l] * (SOFTMAX_SCALE * math.log2(math.e))
    wq_pe = wq[:, :, QK_NOPE:]
    w_q = jnp.concatenate(
        [wq[:, :, :QK_NOPE].reshape(Q_RANK, D_ATTN),
         jnp.concatenate([wq_pe, _swap_halves(wq_pe)], axis=2).reshape(Q_RANK, D_ATTN)], axis=1).astype(BF16)
    wkv = w_ukv[l]
    w_kv = jnp.concatenate([wkv[:, :, :QK_NOPE].reshape(KV_RANK, D_ATTN),
                            wkv[:, :, QK_NOPE:].reshape(KV_RANK, D_ATTN)], axis=1).astype(BF16)

    q, k, vt, z, xbc, dt = _front_call(
        x, meta_tile, norm_mix_pre[l][None], w_in_p, q_a_norm[l][None], w_q,
        kv_a_norm[l][None], w_kv, _rope_table(seq, lp))

    attn = _attn_call(q, k, vt)

    ssm = _ssd_call(z, xbc, dt, ssm_conv_w[l], ssm_conv_b[l][None], _pad_lanes(ssm_dt_bias[l]),
                    _pad_lanes(ssm_A_log[l]), jnp.repeat(ssm_D[l], SSM_P)[None], ssm_norm[l][None])

    cw = jnp.concatenate([ffn_conv_w[l], ffn_conv_b[l][None],
                          jnp.zeros((SUBLANES - FFN_CONV - 1, 2 * D_FF), F32)], axis=0)
    return _tail_call(x, attn, ssm, meta_tokens[N_META - SUBLANES:].astype(F32), attn_out_norm[l][None],
                      w_out[l].astype(BF16), norm_mix_post[l][None], norm_ffn_pre[l][None],
                      w_up[l].astype(BF16), cw, w_down[l].astype(BF16), norm_ffn_post[l][None])
```

```python
import functools
import math

import jax
import jax.numpy as jnp
from jax import lax
from jax.experimental import pallas as pl
from jax.experimental.pallas import tpu as pltpu

F32 = jnp.float32
BF16 = jnp.bfloat16

D_MODEL = 1024
N_META = 16
MLA_HEADS = 8
QK_NOPE = 128
QK_ROPE = 64
V_DIM = 128
Q_RANK = 384
KV_RANK = 256
ROPE_THETA = 10000.0
SOFTMAX_SCALE = (QK_NOPE + QK_ROPE) ** -0.5
D_ATTN = MLA_HEADS * V_DIM
SSM_HEADS = 16
SSM_P = 64
SSM_GROUPS = 2
SSM_N = 128
SSM_CONV = 4
CHUNK = 128
D_SSM = SSM_HEADS * SSM_P
D_XBC = D_SSM + 2 * SSM_GROUPS * SSM_N
D_FF = 2816
FFN_CONV = 3
EPS = 1e-6

LANES = 128
SUBLANES = 8
ROW_TILE = 512
BLK = 128
QK_PAD = 256
FF_CHUNK = 256
FFN_TILE = 512
HALF = ROW_TILE // 2
ATTN_UNROLL = 40
N_FF_CHUNKS = D_FF // FF_CHUNK
LAT_W = 768
IN_W = LAT_W + D_SSM + D_XBC
NEG = -1e30
VMEM_LIMIT = 56 * 1024 * 1024


def _dot(a, b):
    return jnp.dot(a, b, preferred_element_type=F32)


def _dot_nt(a, b):
    return lax.dot_general(a, b, (((1,), (1,)), ((), ())), preferred_element_type=F32)


def _rms(x, gain):
    ms = jnp.mean(x * x, axis=-1, keepdims=True)
    return x * lax.rsqrt(ms + EPS) * gain


def _silu(x):
    h = 0.5 * x
    return h + h * jnp.tanh(h)


def _const_spec(shape):
    nd = len(shape)
    return pl.BlockSpec(shape, lambda *_: (0,) * nd, pipeline_mode=pl.Buffered(1))


def _params():
    return pltpu.CompilerParams(dimension_semantics=("arbitrary", "arbitrary"),
                                vmem_limit_bytes=VMEM_LIMIT)


def _front_kernel(n_real, x_ref, meta_ref, g_ref, win_ref, qan_ref, wq_ref, kvan_ref, wkv_ref, tab_ref,
                  qt_ref, k_ref, vt_ref, z_ref, xbc_ref, dt_ref):
    i = pl.program_id(1)
    half_rows = ROW_TILE // 2
    halves = [slice(k * half_rows, (k + 1) * half_rows) for k in range(2)]
    lane = lax.broadcasted_iota(jnp.int32, (half_rows, LANES), 1)
    low = lane < LANES // 2

    def normed(hs):
        h = jnp.where(i == n_real, meta_ref[hs, :], x_ref[0, hs, :])
        return _rms(h, g_ref[...]).astype(BF16)

    def project(hs, hn):
        lat = _dot(hn, win_ref[:, 0:LAT_W])
        rest = _dot(hn, win_ref[:, LAT_W:IN_W])
        n_groups = (IN_W - LAT_W) // LANES
        swapped = [pltpu.roll(lat[:, LAT_W - LANES:LAT_W], LANES // 2, 1)]
        swapped += [pltpu.roll(rest[:, g * LANES:(g + 1) * LANES], LANES // 2, 1) for g in range(n_groups)]
        group = lambda g: jnp.where(low, swapped[g], swapped[g + 1])
        for g in range(D_SSM // LANES):
            z_ref[0, hs, g * LANES:(g + 1) * LANES] = group(g)
        for g in range(D_XBC // LANES):
            xbc_ref[0, hs, g * LANES:(g + 1) * LANES] = group(D_SSM // LANES + g)
        dt_ref[0, hs, :] = jnp.where(lane < SSM_HEADS, swapped[n_groups], 0.0)
        return lat

    def latent_norms(lat):
        return (_rms(lat[:, 0:Q_RANK], qan_ref[...]).astype(BF16),
                _rms(lat[:, Q_RANK:Q_RANK + KV_RANK], kvan_ref[...]).astype(BF16))

    def store_heads(hs, lat, q, kv):
        tab = tab_ref[hs, :]

        def rope(pair):
            a = pair * tab
            return jnp.where(lane < QK_ROPE, a + pltpu.roll(a, QK_ROPE, 1), 0.0)

        kp = lat[:, Q_RANK + KV_RANK:LAT_W]
        half = QK_ROPE // 2
        kp = jnp.where(lane < QK_ROPE, kp,
                       jnp.where(lane < QK_ROPE + half, pltpu.roll(kp, half, 1), pltpu.roll(kp, QK_ROPE + half, 1)))
        k_pe = rope(kp).astype(BF16)
        for hd in range(MLA_HEADS):
            sl = slice(hd * LANES, (hd + 1) * LANES)
            up = slice(D_ATTN + hd * LANES, D_ATTN + (hd + 1) * LANES)
            qt_ref[0, hd, 0, 0:LANES, hs] = q[:, sl].T.astype(BF16)
            qt_ref[0, hd, 0, LANES:QK_PAD, hs] = rope(q[:, up]).T.astype(BF16)
            k_ref[0, hd, hs, 0:LANES] = kv[:, sl].astype(BF16)
            k_ref[0, hd, hs, LANES:QK_PAD] = k_pe
            vt_ref[0, hd, 0, :, hs] = kv[:, up].T.astype(BF16)

    hn = [normed(hs) for hs in halves]
    lat0 = project(halves[0], hn[0])
    qn0, kvn0 = latent_norms(lat0)
    lat1 = project(halves[1], hn[1])
    q0, kv0 = _dot(qn0, wq_ref[...]), _dot(kvn0, wkv_ref[...])
    qn1, kvn1 = latent_norms(lat1)
    store_heads(halves[0], lat0, q0, kv0)
    q1, kv1 = _dot(qn1, wq_ref[...]), _dot(kvn1, wkv_ref[...])
    store_heads(halves[1], lat1, q1, kv1)


def _front_call(x, meta_tile, gain, w_in, q_an, w_q, kv_an, w_kv, rope_tab):
    bsz, seq, _ = x.shape
    n_real = seq // ROW_TILE
    n_tiles = n_real + 1
    lp = n_tiles * ROW_TILE
    row = lambda w: pl.BlockSpec((1, ROW_TILE, w), lambda b, i: (b, i, 0))
    tab = pl.BlockSpec((ROW_TILE, LANES), lambda b, i: (i, 0))
    qk_spec = pl.BlockSpec((1, MLA_HEADS, ROW_TILE, QK_PAD), lambda b, i: (b, 0, i, 0))
    return pl.pallas_call(
        functools.partial(_front_kernel, n_real),
        grid=(bsz, n_tiles),
        in_specs=[
            pl.BlockSpec((1, ROW_TILE, D_MODEL), lambda b, i: (b, jnp.minimum(i, n_real - 1), 0)),
            _const_spec((ROW_TILE, D_MODEL)),
            _const_spec((1, D_MODEL)),
            _const_spec((D_MODEL, IN_W)),
            _const_spec((1, Q_RANK)),
            _const_spec((Q_RANK, 2 * D_ATTN)),
            _const_spec((1, KV_RANK)),
            _const_spec((KV_RANK, 2 * D_ATTN)),
            tab,
        ],
        out_specs=[
            pl.BlockSpec((1, MLA_HEADS, 1, QK_PAD, ROW_TILE), lambda b, i: (b, 0, i, 0, 0)),
            qk_spec,
            pl.BlockSpec((1, MLA_HEADS, 1, V_DIM, ROW_TILE), lambda b, i: (b, 0, i, 0, 0)),
            row(D_SSM), row(D_XBC), row(LANES),
        ],
        out_shape=[
            jax.ShapeDtypeStruct((bsz, MLA_HEADS, n_tiles, QK_PAD, ROW_TILE), BF16),
            jax.ShapeDtypeStruct((bsz, MLA_HEADS, lp, QK_PAD), BF16),
            jax.ShapeDtypeStruct((bsz, MLA_HEADS, n_tiles, V_DIM, ROW_TILE), BF16),
            jax.ShapeDtypeStruct((bsz, lp, D_SSM), F32),
            jax.ShapeDtypeStruct((bsz, lp, D_XBC), F32),
            jax.ShapeDtypeStruct((bsz, lp, LANES), F32),
        ],
        compiler_params=_params(),
        name="front",
    )(x, meta_tile, gain, w_in, q_an, w_q, kv_an, w_kv, rope_tab)


def _attn_kernel(n_real, qt_ref, k_ref, vt_ref, o_ref, s0_ref, s1_ref, sm0_ref, sm1_ref, c0_ref, c1_ref,
                 m_ref, l_ref, acc_ref):
    pre = n_real * ROW_TILE
    s_refs, sm_refs, c_refs = (s0_ref, s1_ref), (sm0_ref, sm1_ref), (c0_ref, c1_ref)
    k_meta = k_ref[0, 0, pre:pre + N_META, :]
    vt_meta = vt_ref[0, 0, n_real, :, 0:N_META]

    def softmax_cols(s):
        m = jnp.max(s, axis=0, keepdims=True)
        p = jnp.exp2(s - m)
        return m, jnp.sum(p, axis=0, keepdims=True), p

    s = _dot(k_meta, qt_ref[0, 0, n_real, :, 0:BLK])
    kk = lax.broadcasted_iota(jnp.int32, s.shape, 0)
    qq = lax.broadcasted_iota(jnp.int32, s.shape, 1)
    _, l, p = softmax_cols(jnp.where(kk <= qq, s, NEG))
    o_ref[0, pre:pre + BLK, :] = (_dot(vt_meta, p.astype(BF16)) / l).T
    o_ref[0, pre + BLK:, :] = jnp.zeros((ROW_TILE - BLK, V_DIM), F32)

    def run(n_steps, first, advance, diagonal):
        def produce(slot, qi, j):
            qt = qt_ref[0, 0, jnp.minimum(qi, n_real - 1)]
            ks = pl.multiple_of(jnp.minimum(j, n_real - 1) * ROW_TILE, ROW_TILE)
            if diagonal:
                s_lo = _dot(k_ref[0, 0, pl.ds(ks, HALF), :], qt)
                s_hi = _dot(k_ref[0, 0, pl.ds(ks + HALF, HALF), :], qt[:, HALF:])
                s_lo = jnp.where(lax.broadcasted_iota(jnp.int32, s_lo.shape, 0)
                                 <= lax.broadcasted_iota(jnp.int32, s_lo.shape, 1), s_lo, NEG)
                s_hi = jnp.where(lax.broadcasted_iota(jnp.int32, s_hi.shape, 0)
                                 <= lax.broadcasted_iota(jnp.int32, s_hi.shape, 1), s_hi, NEG)
                s_meta = _dot(k_meta, qt)
                sm_refs[slot][...] = s_meta
                s_refs[slot][0:HALF, :] = s_lo
                s_refs[slot][HALF:, HALF:] = s_hi
                c = jnp.maximum(jnp.max(s_meta, axis=0, keepdims=True), jnp.max(s_lo, axis=0, keepdims=True))
                c_refs[slot][...] = jnp.concatenate(
                    [c[:, :HALF], jnp.maximum(c[:, HALF:], jnp.max(s_hi, axis=0, keepdims=True))], axis=1)
            else:
                s = _dot(k_ref[0, 0, pl.ds(ks, ROW_TILE), :], qt)
                s_refs[slot][...] = s
                c_refs[slot][...] = jnp.max(s, axis=0, keepdims=True)

        def step(slot, qi, j):
            nqi, nj = advance(qi, j)
            produce(1 - slot, nqi, nj)
            m_prev = m_ref[qi]
            m_new = jnp.maximum(m_prev, c_refs[slot][...])
            alpha = jnp.exp2(m_prev - m_new)
            if diagonal:
                p_lo = jnp.exp2(s_refs[slot][0:HALF, :] - m_new)
                p_hi = jnp.exp2(s_refs[slot][HALF:, HALF:] - m_new[:, HALF:])
                l = alpha * l_ref[qi] + jnp.sum(p_lo, axis=0, keepdims=True)
                acc = alpha * acc_ref[qi] + _dot(vt_ref[0, 0, j, :, 0:HALF], p_lo.astype(BF16))
                l = jnp.concatenate([l[:, :HALF], l[:, HALF:] + jnp.sum(p_hi, axis=0, keepdims=True)], axis=1)
                acc = jnp.concatenate(
                    [acc[:, :HALF], acc[:, HALF:] + _dot(vt_ref[0, 0, j, :, HALF:], p_hi.astype(BF16))], axis=1)
            else:
                p = jnp.exp2(s_refs[slot][...] - m_new)
                l = alpha * l_ref[qi] + jnp.sum(p, axis=0, keepdims=True)
                acc = alpha * acc_ref[qi] + _dot(vt_ref[0, 0, j], p.astype(BF16))
            if diagonal:
                p_meta = jnp.exp2(sm_refs[slot][...] - m_new)
                l = l + jnp.sum(p_meta, axis=0, keepdims=True)
                acc = acc + _dot(vt_meta, p_meta.astype(BF16))
                qs = pl.multiple_of(qi * ROW_TILE, ROW_TILE)
                o_ref[0, pl.ds(qs, ROW_TILE), :] = (acc / l).T
            else:
                m_ref[qi] = m_new
                l_ref[qi] = l
                acc_ref[qi] = acc
            return nqi, nj

        def body(_, carry):
            for u in range(ATTN_UNROLL):
                carry = step(u % 2, *carry)
            return carry

        if n_steps > 0:
            first = (jnp.int32(first[0]), jnp.int32(first[1]))
            produce(0, *first)
            carry = lax.fori_loop(0, n_steps // ATTN_UNROLL, body, first)
            for u in range(n_steps % ATTN_UNROLL):
                carry = step(u % 2, *carry)

    m_ref[...] = jnp.full(m_ref.shape, NEG, F32)
    l_ref[...] = jnp.zeros(l_ref.shape, F32)
    acc_ref[...] = jnp.zeros(acc_ref.shape, F32)

    def below(qi, j):
        wrap = j + 1 == qi
        return jnp.where(wrap, qi + 1, qi), jnp.where(wrap, 0, j + 1)

    run(n_real * (n_real - 1) // 2, (1, 0), below, False)
    run(n_real, (0, 0), lambda qi, j: (qi + 1, j + 1), True)


def _attn_call(qt, k, vt):
    bsz, _, lp, _ = k.shape
    n_real = lp // ROW_TILE - 1
    tiles = lambda rows: pl.BlockSpec((1, 1, n_real + 1, rows, ROW_TILE), lambda b, h: (b, h, 0, 0, 0))
    return pl.pallas_call(
        functools.partial(_attn_kernel, n_real),
        grid=(bsz, MLA_HEADS),
        in_specs=[tiles(QK_PAD), pl.BlockSpec((1, 1, lp, QK_PAD), lambda b, h: (b, h, 0, 0)), tiles(V_DIM)],
        out_specs=pl.BlockSpec((1, lp, V_DIM), lambda b, h: (b, 0, h)),
        out_shape=jax.ShapeDtypeStruct((bsz, lp, D_ATTN), F32),
        scratch_shapes=[
            pltpu.VMEM((ROW_TILE, ROW_TILE), F32), pltpu.VMEM((ROW_TILE, ROW_TILE), F32),
            pltpu.VMEM((N_META, ROW_TILE), F32), pltpu.VMEM((N_META, ROW_TILE), F32),
            pltpu.VMEM((1, ROW_TILE), F32), pltpu.VMEM((1, ROW_TILE), F32),
            pltpu.VMEM((n_real, 1, ROW_TILE), F32), pltpu.VMEM((n_real, 1, ROW_TILE), F32),
            pltpu.VMEM((n_real, V_DIM, ROW_TILE), F32),
        ],
        compiler_params=_params(),
        name="attn",
    )(qt, k, vt)


N_PAIRS = SSM_HEADS // 2


def _split3(a):
    a1 = a.astype(BF16)
    r1 = a - a1.astype(F32)
    a2 = r1.astype(BF16)
    a3 = (r1 - a2.astype(F32)).astype(BF16)
    return a1, a2, a3


def _ssd_kernel(n_chunks, z_ref, xbc_ref, dt_ref, cw_ref, cb_ref, dtb_ref, alog_ref, dskip_ref, ng_ref,
                y_ref, xtail_ref, state_ref):
    c = pl.program_id(1)
    bsz = z_ref.shape[0]

    @pl.when(c == 0)
    def _():
        xtail_ref[...] = jnp.zeros(xtail_ref.shape, F32)
        state_ref[...] = jnp.zeros(state_ref.shape, F32)

    @pl.when(c >= n_chunks)
    def _():
        y_ref[...] = jnp.zeros(y_ref.shape, F32)

    def chunk(b):
        x = xbc_ref[b]
        tail = xtail_ref[b]
        first_row = lax.broadcasted_iota(jnp.int32, (SUBLANES, D_XBC), 0) == 0

        def delayed(t, t_tail):
            r = pltpu.roll(t, 1, 0)
            head = jnp.where(first_row, pltpu.roll(t_tail, 1, 0), r[0:SUBLANES])
            return jnp.concatenate([head, r[SUBLANES:]], axis=0)

        conv, conv_tail = cw_ref[0:1, :] * x, cw_ref[0:1, :] * tail
        for k in range(1, SSM_CONV):
            conv = cw_ref[k:k + 1, :] * x + delayed(conv, conv_tail)
            conv_tail = cw_ref[k:k + 1, :] * tail + pltpu.roll(conv_tail, 1, 0)
        xtail_ref[b] = jnp.where(c == 0, x[N_META - SUBLANES:N_META, :], x[CHUNK - SUBLANES:CHUNK, :])
        xc = _silu(conv + cb_ref[...])
        xs = xc[:, 0:D_SSM]

        row = lax.broadcasted_iota(jnp.int32, (CHUNK, LANES), 0)
        col = lax.broadcasted_iota(jnp.int32, (CHUNK, LANES), 1)
        dt = jax.nn.softplus(dt_ref[b] + dtb_ref[...])
        dt = jnp.where((c > 0) | (row < N_META), dt, 0.0)
        a = dt * (-jnp.exp(alog_ref[...]))
        tri = (col <= row).astype(BF16)
        a1, a2, a3 = _split3(a)
        acs = (_dot(tri, a1) + _dot(tri, a2) + _dot(tri, a3)) * math.log2(math.e)
        acs_t = acs.T
        dt_t = dt.T
        e_acs = jnp.exp2(acs)
        last_t = acs_t[:, CHUNK - 1:CHUNK]
        w_t = dt_t * jnp.exp2(last_t - acs_t)
        dec_t = jnp.exp2(last_t)
        causal = col <= row
        left = col < SSM_P

        for g in range(SSM_GROUPS):
            bm = xc[:, D_SSM + g * SSM_N:D_SSM + (g + 1) * SSM_N]
            cm = xc[:, D_SSM + SSM_GROUPS * SSM_N + g * SSM_N:D_SSM + SSM_GROUPS * SSM_N + (g + 1) * SSM_N]
            bm16 = bm.astype(BF16)
            cm16 = cm.astype(BF16)
            cb = _dot_nt(cm16, bm16)
            bm_t = bm.T
            for pp in range(N_PAIRS // SSM_GROUPS):
                pair = g * (N_PAIRS // SSM_GROUPS) + pp
                xs_pair = xs[:, pair * LANES:(pair + 1) * LANES]
                prev = state_ref[b, pair]
                rhs_x = [jnp.where(left, xs_pair, 0.0).astype(BF16), jnp.where(left, 0.0, xs_pair).astype(BF16)]
                rhs_s = [jnp.where(left, prev, 0.0).astype(BF16), jnp.where(left, 0.0, prev).astype(BF16)]
                lhs_m, lhs_c, lhs_b, decs = [], [], [], []
                for k in range(2):
                    hd = 2 * pair + k
                    seg = acs[:, hd:hd + 1] - acs_t[hd:hd + 1, :]
                    lm = jnp.exp2(jnp.where(causal, seg, NEG))
                    lhs_m.append((cb * lm * dt_t[hd:hd + 1, :]).astype(BF16))
                    lhs_c.append((cm * e_acs[:, hd:hd + 1]).astype(BF16))
                    lhs_b.append((bm_t * w_t[hd:hd + 1, :]).astype(BF16))
                    decs.append(jnp.broadcast_to(dec_t[hd:hd + 1, :], (SSM_N, LANES)))
                y_pair = _dot(jnp.concatenate(lhs_m + lhs_c, axis=1), jnp.concatenate(rhs_x + rhs_s, axis=0))
                new = _dot(jnp.concatenate(lhs_b, axis=1), jnp.concatenate(rhs_x, axis=0))
                state_ref[b, pair] = jnp.where(left, decs[0], decs[1]) * prev + new
                y_ref[b, :, pair * LANES:(pair + 1) * LANES] = y_pair

        y = (y_ref[b] + dskip_ref[...] * xs) * _silu(z_ref[b])
        gsz = D_SSM // SSM_GROUPS
        for g in range(SSM_GROUPS):
            yg = y[:, g * gsz:(g + 1) * gsz]
            ms = jnp.mean(yg * yg, axis=-1, keepdims=True)
            y_ref[b, :, g * gsz:(g + 1) * gsz] = yg * lax.rsqrt(ms + EPS) * ng_ref[:, g * gsz:(g + 1) * gsz]

    @pl.when(c < n_chunks)
    def _():
        for b in range(bsz):
            chunk(b)


def _ssd_call(z, xbc, dt, conv_w, conv_b, dt_bias, a_log, d_skip, norm_gain):
    bsz, lp, _ = z.shape
    n_blocks = lp // CHUNK
    pre_blk = (lp - ROW_TILE) // CHUNK
    n_chunks = pre_blk + 1

    def blk(_, c):
        return (0, jnp.where(c == 0, pre_blk, jnp.where(c < n_chunks, c - 1, c)), 0)

    row = lambda w: pl.BlockSpec((bsz, CHUNK, w), blk)
    return pl.pallas_call(
        functools.partial(_ssd_kernel, n_chunks),
        grid=(1, n_blocks),
        in_specs=[row(D_SSM), row(D_XBC), row(LANES),
                  _const_spec((SSM_CONV, D_XBC)), _const_spec((1, D_XBC)), _const_spec((1, LANES)),
                  _const_spec((1, LANES)), _const_spec((1, D_SSM)), _const_spec((1, D_SSM))],
        out_specs=row(D_SSM),
        out_shape=jax.ShapeDtypeStruct((bsz, lp, D_SSM), F32),
        scratch_shapes=[pltpu.VMEM((bsz, SUBLANES, D_XBC), F32),
                        pltpu.VMEM((bsz, N_PAIRS, SSM_N, LANES), F32)],
        compiler_params=_params(),
        name="ssd",
    )(z, xbc, dt, conv_w, conv_b, dt_bias, a_log, d_skip, norm_gain)


def _shifted(u, tail, d):
    r = pltpu.roll(u, d, 0)
    row = lax.broadcasted_iota(jnp.int32, (SUBLANES, u.shape[1]), 0)
    head = jnp.where(row < d, pltpu.roll(tail, d, 0), r[0:SUBLANES])
    return jnp.concatenate([head, r[SUBLANES:]], axis=0)


def _tail_kernel(x_ref, attn_ref, ssm_ref, xm_ref, attnm_ref, ssmm_ref, ga_ref, wout_ref, gmix_ref,
                 gpre_ref, wu_ref, cw_ref, wd_ref, gpost_ref, out_ref, tail_ref):
    i = pl.program_id(1)
    g_pre = gpre_ref[...]
    cols = lambda c, half: slice(half * D_FF + c * FF_CHUNK, half * D_FF + (c + 1) * FF_CHUNK)

    def mixed(h, attn, ssm):
        an = _rms(attn, ga_ref[...]).astype(BF16)
        mix = _dot(an, wout_ref[0:D_ATTN, :]) + _dot(ssm.astype(BF16), wout_ref[D_ATTN:, :])
        return h + _rms(mix, gmix_ref[...])

    @pl.when(i == 0)
    def _():
        rows = slice(N_META - SUBLANES, N_META)
        hm = _rms(mixed(xm_ref[...], attnm_ref[0, rows, :], ssmm_ref[0, rows, :]), g_pre).astype(BF16)
        for c in range(2 * N_FF_CHUNKS):
            sl = slice(c * FF_CHUNK, (c + 1) * FF_CHUNK)
            tail_ref[:, sl] = _dot(hm, wu_ref[:, sl])

    half_rows = FFN_TILE // 2
    halves = [slice(k * half_rows, (k + 1) * half_rows) for k in range(2)]
    h1 = [mixed(x_ref[0, hs, :], attn_ref[0, hs, :], ssm_ref[0, hs, :]) for hs in halves]
    hn = [_rms(h, g_pre).astype(BF16) for h in h1]

    def conv(u, sl):
        tails = [tail_ref[:, sl], u[0][half_rows - SUBLANES:half_rows, :]]
        tail_ref[:, sl] = u[1][half_rows - SUBLANES:half_rows, :]
        return [cw_ref[FFN_CONV:FFN_CONV + 1, sl] + cw_ref[2:3, sl] * uk + cw_ref[1:2, sl] * _shifted(uk, tk, 1)
                + cw_ref[0:1, sl] * _shifted(uk, tk, 2) for uk, tk in zip(u, tails)]

    def up(c):
        return [[_dot(hn[k], wu_ref[:, cols(c, part)]) for k in range(2)] for part in range(2)]

    def gate(c, u):
        g, v = conv(u[0], cols(c, 0)), conv(u[1], cols(c, 1))
        return [(_silu(g[k]) * v[k]).astype(BF16) for k in range(2)]

    u = {0: up(0), 1: up(1)}
    act = gate(0, u.pop(0))
    down = [None, None]
    for c in range(N_FF_CHUNKS):
        if c + 2 < N_FF_CHUNKS:
            u[c + 2] = up(c + 2)
        for k in range(2):
            d = _dot(act[k], wd_ref[c * FF_CHUNK:(c + 1) * FF_CHUNK, :])
            down[k] = d if down[k] is None else down[k] + d
        if c + 1 < N_FF_CHUNKS:
            act = gate(c + 1, u.pop(c + 1))
    for k in range(2):
        out_ref[0, halves[k], :] = h1[k] + _rms(down[k], gpost_ref[...])


def _tail_call(x, attn, ssm, meta_rows, gain_attn, w_out, gain_mix, gain_pre, w_up, conv_wb, w_down, gain_post):
    bsz, seq, _ = x.shape
    row = pl.BlockSpec((1, FFN_TILE, D_MODEL), lambda b, i: (b, i, 0))
    prefix = pl.BlockSpec((1, BLK, D_MODEL), lambda b, i: (b, seq // BLK, 0))
    vec = _const_spec((1, D_MODEL))
    return pl.pallas_call(
        _tail_kernel,
        grid=(bsz, seq // FFN_TILE),
        in_specs=[row, row, row, _const_spec((SUBLANES, D_MODEL)), prefix, prefix,
                  vec, _const_spec((D_ATTN + D_SSM, D_MODEL)), vec, vec,
                  _const_spec((D_MODEL, 2 * D_FF)), _const_spec((SUBLANES, 2 * D_FF)),
                  _const_spec((D_FF, D_MODEL)), vec],
        out_specs=row,
        out_shape=jax.ShapeDtypeStruct((bsz, seq, D_MODEL), F32),
        scratch_shapes=[pltpu.VMEM((SUBLANES, 2 * D_FF), F32)],
        compiler_params=_params(),
        name="tail",
    )(x, attn, ssm, meta_rows, attn, ssm, gain_attn, w_out, gain_mix, gain_pre, w_up, conv_wb, w_down, gain_post)


def _swap_halves(t):
    half = t.shape[-1] // 2
    return jnp.concatenate([t[..., half:], t[..., :half]], axis=-1)


def _rope_table(seq, lp):
    inv = ROPE_THETA ** (-jnp.arange(0, QK_ROPE, 2, dtype=F32) / QK_ROPE)

    def cs(pos):
        ang = pos.astype(F32)[:, None] * inv[None, :]
        return jnp.cos(ang), jnp.sin(ang)

    c_off, s_off = cs(N_META + ROW_TILE * jnp.arange(seq // ROW_TILE))
    c_row, s_row = cs(jnp.arange(ROW_TILE))
    cos = (c_off[:, None] * c_row[None] - s_off[:, None] * s_row[None]).reshape(seq, -1)
    sin = (s_off[:, None] * c_row[None] + c_off[:, None] * s_row[None]).reshape(seq, -1)
    c_meta, s_meta = cs(jnp.arange(N_META))
    tab = lambda c, s: jnp.concatenate([c, c, -s, s], axis=1)
    return jnp.concatenate([tab(cos, sin), tab(c_meta, s_meta),
                            jnp.zeros((lp - seq - N_META, LANES), F32)], axis=0)


def _pad_lanes(v, width=LANES):
    return jnp.zeros((1, width), F32).at[0, :v.shape[0]].set(v)


def kernel(x, meta_tokens, norm_mix_pre, norm_mix_post, norm_ffn_pre, norm_ffn_post, w_in, q_a_norm, w_uq,
           kv_a_norm, w_ukv, attn_out_norm, ssm_conv_w, ssm_conv_b, ssm_dt_bias, ssm_A_log, ssm_D, ssm_norm,
           w_out, w_up, ffn_conv_w, ffn_conv_b, w_down):
    bsz, seq, _ = x.shape
    lp = seq + ROW_TILE
    l = 0
    meta_tile = jnp.zeros((ROW_TILE, D_MODEL), F32).at[0:N_META].set(meta_tokens.astype(F32))

    w_in_p = jnp.pad(w_in[l].astype(BF16), ((0, 0), (0, IN_W - w_in.shape[-1])))

    wq = w_uq[l] * (SOFTMAX_SCALE * math.log2(math.e))
    wq_pe = wq[:, :, QK_NOPE:]
    w_q = jnp.concatenate(
        [wq[:, :, :QK_NOPE].reshape(Q_RANK, D_ATTN),
         jnp.concatenate([wq_pe, _swap_halves(wq_pe)], axis=2).reshape(Q_RANK, D_ATTN)], axis=1).astype(BF16)
    wkv = w_ukv[l]
    w_kv = jnp.concatenate([wkv[:, :, :QK_NOPE].reshape(KV_RANK, D_ATTN),
                            wkv[:, :, QK_NOPE:].reshape(KV_RANK, D_ATTN)], axis=1).astype(BF16)

    q, k, vt, z, xbc, dt = _front_call(
        x, meta_tile, norm_mix_pre[l][None], w_in_p, q_a_norm[l][None], w_q,
        kv_a_norm[l][None], w_kv, _rope_table(seq, lp))

    attn = _attn_call(q, k, vt)

    ssm = _ssd_call(z, xbc, dt, ssm_conv_w[l], ssm_conv_b[l][None], _pad_lanes(ssm_dt_bias[l]),
                    _pad_lanes(ssm_A_log[l]), jnp.repeat(ssm_D[l], SSM_P)[None], ssm_norm[l][None])

    cw = jnp.concatenate([ffn_conv_w[l], ffn_conv_b[l][None],
                          jnp.zeros((SUBLANES - FFN_CONV - 1, 2 * D_FF), F32)], axis=0)
    return _tail_call(x, attn, ssm, meta_tokens[N_META - SUBLANES:].astype(F32), attn_out_norm[l][None],
                      w_out[l].astype(BF16), norm_mix_post[l][None], norm_ffn_pre[l][None],
                      w_up[l].astype(BF16), cw, w_down[l].astype(BF16), norm_ffn_post[l][None])
```
